```python
import math
import jax, jax.numpy as jnp
from jax import lax
import numpy as np

D_MODEL = 1024
BATCH = 16
SEQ = 2048
DEPTH = 1

HEAD_DIM = 64
N_HEADS_FOX = 8
N_HEADS_SB = 8
D_FOX = N_HEADS_FOX * HEAD_DIM
D_SB = N_HEADS_SB * HEAD_DIM
D_FF = 4 * D_MODEL
D_PLE = 256
Q_BLOCK = 128
EPS = 1e-6
IN_SIZES = (D_FOX, D_FOX, D_FOX, N_HEADS_FOX, D_SB, D_SB, D_SB, D_MODEL, D_MODEL)
D_IN = 3 * D_FOX + N_HEADS_FOX + 3 * D_SB + 2 * D_MODEL

kernel_name = "fox_stickbreaking_gated_hybrid_block"


def _rmsnorm(x, g):
    xf = x.astype(jnp.float32)
    r = lax.rsqrt(jnp.mean(xf * xf, axis=-1, keepdims=True) + EPS)
    return (xf * r * g.astype(jnp.float32)).astype(x.dtype)


def _split_cols(u):
    offs = []
    o = 0
    for s in IN_SIZES[:-1]:
        o += s
        offs.append(o)
    return jnp.split(u, offs, axis=-1)


def _heads(u, n_heads):
    b, s, _ = u.shape
    return u.reshape(b, s, n_heads, HEAD_DIM).transpose(0, 2, 1, 3).astype(jnp.float32)


def _merge_heads(o, dtype):
    b, h, s, d = o.shape
    return o.transpose(0, 2, 1, 3).reshape(b, s, h * d).astype(dtype)


def _forgetting_attention(q, k, v, log_f):
    s_len = q.shape[2]
    scale = HEAD_DIM ** -0.5
    c = jnp.cumsum(log_f, axis=-1)
    outs = []
    for blk in range(s_len // Q_BLOCK):
        q0 = blk * Q_BLOCK
        k_end = q0 + Q_BLOCK
        qb = q[:, :, q0:k_end]
        kb = k[:, :, :k_end]
        vb = v[:, :, :k_end]
        logits = (jnp.einsum('bhqd,bhkd->bhqk', qb, kb) * scale
                  + c[:, :, q0:k_end, None] - c[:, :, None, :k_end])
        q_pos = q0 + jnp.arange(Q_BLOCK)
        k_pos = jnp.arange(k_end)
        causal = k_pos[None, :] <= q_pos[:, None]
        logits = jnp.where(causal, logits, -jnp.inf)
        probs = jax.nn.softmax(logits, axis=-1)
        outs.append(jnp.einsum('bhqk,bhkd->bhqd', probs, vb))
    return jnp.concatenate(outs, axis=2)


def _stick_breaking_attention(q, k, v):
    s_len = q.shape[2]
    scale = HEAD_DIM ** -0.5
    outs = []
    for blk in range(s_len // Q_BLOCK):
        q0 = blk * Q_BLOCK
        k_end = q0 + Q_BLOCK
        qb = q[:, :, q0:k_end]
        kb = k[:, :, :k_end]
        vb = v[:, :, :k_end]
        z = jnp.einsum('bhqd,bhkd->bhqk', qb, kb) * scale
        q_pos = q0 + jnp.arange(Q_BLOCK)
        k_pos = jnp.arange(k_end)
        strict = k_pos[None, :] < q_pos[:, None]
        log_1m_beta = jnp.where(strict, jax.nn.log_sigmoid(-z), 0.0)
        tail = lax.cumsum(log_1m_beta, axis=3, reverse=True) - log_1m_beta
        weights = jnp.where(strict, jnp.exp(jax.nn.log_sigmoid(z) + tail), 0.0)
        outs.append(jnp.einsum('bhqk,bhkd->bhqd', weights, vb))
    return jnp.concatenate(outs, axis=2)


def setup_inputs(seed: int = 0) -> dict:
    key = jax.random.key(seed)
    ks = jax.random.split(key, 16)
    f32 = jnp.float32

    def w(k, shape, fan_in, gain=1.0):
        return jax.random.normal(k, shape, f32) * (gain * fan_in ** -0.5)

    def gain(k, shape):
        return 1.0 + 0.02 * jax.random.normal(k, shape, f32)

    return {
        "x": jax.random.normal(ks[0], (BATCH, SEQ, D_MODEL), f32),
        "p": jax.random.normal(ks[1], (DEPTH, BATCH, SEQ, D_PLE), f32),
        "g_mix": gain(ks[2], (DEPTH, D_MODEL)),
        "w_in": w(ks[3], (DEPTH, D_MODEL, D_IN), D_MODEL),
        "b_forget": jax.random.uniform(ks[4], (DEPTH, N_HEADS_FOX), f32, 1.0, 4.0),
        "b_gate": 0.01 * jax.random.normal(ks[5], (DEPTH, 2, D_MODEL), f32),
        "w_branch_fox": w(ks[6], (DEPTH, D_FOX, D_MODEL), D_FOX),
        "w_branch_sb": w(ks[7], (DEPTH, D_SB, D_MODEL), D_SB),
        "w_out": w(ks[8], (DEPTH, D_MODEL, D_MODEL), D_MODEL),
        "g_mlp": gain(ks[9], (DEPTH, D_MODEL)),
        "w_up": w(ks[10], (DEPTH, D_MODEL, D_FF), D_MODEL),
        "w_down": w(ks[11], (DEPTH, D_FF, D_MODEL), D_FF, gain=0.5),
        "g_ple": gain(ks[12], (DEPTH, D_MODEL)),
        "w_ple_gate": w(ks[13], (DEPTH, D_MODEL, D_MODEL), D_MODEL),
        "w_ple": w(ks[14], (DEPTH, D_PLE, D_MODEL), D_PLE),
        "g_final": gain(ks[15], (D_MODEL,)),
    }


def reference(x, p, g_mix, w_in, b_forget, b_gate, w_branch_fox, w_branch_sb, w_out,
              g_mlp, w_up, w_down, g_ple, w_ple_gate, w_ple, g_final):
    b, s, _ = x.shape
    for i in range(DEPTH):
        h = _rmsnorm(x, g_mix[i])
        u = h @ w_in[i]
        q_a, k_a, v_a, f_a, q_b, k_b, v_b, gl_a, gl_b = _split_cols(u)
        log_f = jax.nn.log_sigmoid((f_a + b_forget[i]).astype(jnp.float32)).transpose(0, 2, 1)
        o_fox = _forgetting_attention(_heads(q_a, N_HEADS_FOX), _heads(k_a, N_HEADS_FOX),
                                      _heads(v_a, N_HEADS_FOX), log_f)
        o_sb = _stick_breaking_attention(_heads(q_b, N_HEADS_SB), _heads(k_b, N_HEADS_SB),
                                         _heads(v_b, N_HEADS_SB))
        o_fox = _merge_heads(o_fox, x.dtype) @ w_branch_fox[i]
        o_sb = _merge_heads(o_sb, x.dtype) @ w_branch_sb[i]
        merged = (jax.nn.sigmoid(gl_a + b_gate[i, 0]) * o_fox
                  + jax.nn.sigmoid(gl_b + b_gate[i, 1]) * o_sb)
        x = x + merged @ w_out[i]
        h = _rmsnorm(x, g_mlp[i])
        x = x + jnp.square(jax.nn.relu(h @ w_up[i])) @ w_down[i]
        h = _rmsnorm(x, g_ple[i])
        x = x + jax.nn.sigmoid(h @ w_ple_gate[i]) * (p[i] @ w_ple[i])
    return _rmsnorm(x, g_final)
```

```python
import functools

import jax
import jax.numpy as jnp
from jax import lax
from jax.experimental import pallas as pl
from jax.experimental.pallas import tpu as pltpu

D_MODEL = 1024
HEAD_DIM = 64
N_HEADS = 8
D_ATT = N_HEADS * HEAD_DIM
D_FF = 4 * D_MODEL
D_PLE = 256
EPS = 1e-6
SCALE = HEAD_DIM ** -0.5

LANES = 128
SUBLANES = 8
HEADS_PER_STEP = LANES // HEAD_DIM
N_PAIRS = N_HEADS // HEADS_PER_STEP
NEG_BIG = -1e30

TM_PROJ = 512
TM_POST = 512
FF_CHUNK = 512
TQ = 256
TK = 256
CUM_BLK = 128

MIB = 1024 * 1024
BF16 = jnp.bfloat16
F32 = jnp.float32

_NT = (((1,), (1,)), ((), ()))


def _const_spec(shape):
    return pl.BlockSpec(shape, lambda *_: (0,) * len(shape), pipeline_mode=pl.Buffered(1))


def _rms_scale(x, g):
    r = lax.rsqrt(jnp.mean(x * x, axis=-1, keepdims=True) + EPS)
    return x * r * g


def _in_proj_kernel(x_ref, g_ref, wa_ref, wf_ref, wb_ref, wg_ref,
                    qkva_ref, f_ref, qkvb_ref, gl_ref):
    h = _rms_scale(x_ref[...], g_ref[...]).astype(BF16)
    qkva_ref[...] = jnp.dot(h, wa_ref[...], preferred_element_type=F32).astype(BF16)
    f_ref[...] = jnp.dot(h, wf_ref[...], preferred_element_type=F32)
    qkvb_ref[...] = jnp.dot(h, wb_ref[...], preferred_element_type=F32).astype(BF16)
    gl_ref[...] = jnp.dot(h, wg_ref[...], preferred_element_type=F32).astype(BF16)


def _in_proj(x2, g_mix, wa, wf, wb, wg):
    t = x2.shape[0]
    tm = TM_PROJ
    row = lambda n: pl.BlockSpec((tm, n), lambda i: (i, 0))
    w_bytes = 2 * D_MODEL * (wa.shape[1] + wf.shape[1] + wb.shape[1] + wg.shape[1])
    tile_bytes = tm * (4 * D_MODEL + 2 * wa.shape[1] + 4 * wf.shape[1] + 2 * wb.shape[1] + 2 * wg.shape[1])
    scratch_bytes = 4 * tm * wg.shape[1] + 2 * tm * D_MODEL
    return pl.pallas_call(
        _in_proj_kernel,
        grid=(t // tm,),
        in_specs=[row(D_MODEL), _const_spec(g_mix.shape), _const_spec(wa.shape), _const_spec(wf.shape),
                  _const_spec(wb.shape), _const_spec(wg.shape)],
        out_specs=[row(wa.shape[1]), row(wf.shape[1]), row(wb.shape[1]), row(wg.shape[1])],
        out_shape=[jax.ShapeDtypeStruct((t, wa.shape[1]), BF16),
                   jax.ShapeDtypeStruct((t, wf.shape[1]), F32),
                   jax.ShapeDtypeStruct((t, wb.shape[1]), BF16),
                   jax.ShapeDtypeStruct((t, wg.shape[1]), BF16)],
        compiler_params=pltpu.CompilerParams(
            dimension_semantics=("arbitrary",),
            vmem_limit_bytes=w_bytes + 2 * tile_bytes + 2 * scratch_bytes),
        name="in_proj",
    )(x2, g_mix, wa, wf, wb, wg)


def _split3(a):
    a1 = a.astype(BF16)
    r1 = a - a1.astype(F32)
    a2 = r1.astype(BF16)
    a3 = (r1 - a2.astype(F32)).astype(BF16)
    return a1, a2, a3


def _forget_cumsum_kernel(f_ref, b_ref, ctok_ref, ct_ref):
    s_len = f_ref.shape[1]
    row = lax.broadcasted_iota(jnp.int32, (CUM_BLK, CUM_BLK), 0)
    col = lax.broadcasted_iota(jnp.int32, (CUM_BLK, CUM_BLK), 1)
    tri = (col <= row).astype(BF16)
    carry = jnp.zeros((1, LANES), F32)
    for blk in range(s_len // CUM_BLK):
        rows = pl.ds(blk * CUM_BLK, CUM_BLK)
        y = f_ref[0, rows, :] + b_ref[...]
        log_f = -(jnp.maximum(-y, 0.0) + jnp.log(1.0 + jnp.exp(-jnp.abs(y))))
        c = carry
        for part in _split3(log_f):
            c = c + jnp.dot(tri, part, preferred_element_type=F32)
        carry = c[CUM_BLK - 1:CUM_BLK, :]
        ctok_ref[0, rows, :] = c
        ct_ref[0, :, rows] = jnp.transpose(c)[:SUBLANES, :]


def _forget_cumsum(f3, b_pad):
    b, s_len, _ = f3.shape
    return pl.pallas_call(
        _forget_cumsum_kernel,
        grid=(b,),
        in_specs=[pl.BlockSpec((1, s_len, LANES), lambda i: (i, 0, 0)), _const_spec(b_pad.shape)],
        out_specs=[pl.BlockSpec((1, s_len, LANES), lambda i: (i, 0, 0)),
                   pl.BlockSpec((1, SUBLANES, s_len), lambda i: (i, 0, 0))],
        out_shape=[jax.ShapeDtypeStruct((b, s_len, LANES), F32),
                   jax.ShapeDtypeStruct((b, SUBLANES, s_len), F32)],
        compiler_params=pltpu.CompilerParams(dimension_semantics=("arbitrary",)),
        name="forget_cumsum",
    )(f3, b_pad)


def _split_heads(q_ref):
    lane = lax.broadcasted_iota(jnp.int32, (TQ, LANES), 1)
    first = lane < HEAD_DIM
    q2 = q_ref[0] * jnp.asarray(SCALE, BF16)
    zero = jnp.zeros_like(q2)
    return first, (jnp.where(first, q2, zero), jnp.where(first, zero, q2))


def _attn_specs(s_len):
    q_spec = pl.BlockSpec((1, TQ, LANES), lambda b, p, i: (b, i, p))
    k_spec = pl.BlockSpec((1, s_len, LANES), lambda b, p, i: (b, 0, N_PAIRS + p))
    v_spec = pl.BlockSpec((1, s_len, LANES), lambda b, p, i: (b, 0, 2 * N_PAIRS + p))
    o_spec = pl.BlockSpec((1, TQ, LANES), lambda b, p, i: (b, i, p))
    return q_spec, k_spec, v_spec, o_spec


def _fox_kernel(q_ref, k_ref, v_ref, ctok_ref, ct_ref, o_ref, m_ref, l_ref, acc_ref):
    pair = pl.program_id(1)
    qi = pl.program_id(2)
    first, qh = _split_heads(q_ref)
    lane = lax.broadcasted_iota(jnp.int32, (TQ, LANES), 1)
    sub = lax.broadcasted_iota(jnp.int32, (SUBLANES, TK), 0)
    row = lax.broadcasted_iota(jnp.int32, (TQ, TK), 0)
    col = lax.broadcasted_iota(jnp.int32, (TQ, TK), 1)
    ctok = ctok_ref[0]
    c_q = [jnp.sum(jnp.where(lane == HEADS_PER_STEP * pair + i, ctok, 0.0), axis=-1, keepdims=True)
           for i in range(HEADS_PER_STEP)]

    m_ref[...] = jnp.full(m_ref.shape, NEG_BIG, F32)
    l_ref[...] = jnp.zeros(l_ref.shape, F32)
    acc_ref[...] = jnp.zeros(acc_ref.shape, F32)

    def block(j, masked):
        keys = pl.ds(pl.multiple_of(j * TK, TK), TK)
        k2 = k_ref[0, keys, :]
        v2 = v_ref[0, keys, :]
        c8 = ct_ref[0, :, keys]
        for i in range(HEADS_PER_STEP):
            c_k = jnp.sum(jnp.where(sub == HEADS_PER_STEP * pair + i, c8, 0.0), axis=0, keepdims=True)
            s = lax.dot_general(qh[i], k2, _NT, preferred_element_type=F32)
            u = s - c_k
            if masked:
                u = jnp.where(col <= row, u, NEG_BIG)
            m_old = m_ref[i]
            m_new = jnp.maximum(m_old, jnp.max(u, axis=-1, keepdims=True) + c_q[i])
            alpha = jnp.exp(m_old - m_new)
            prob = jnp.exp(u - (m_new - c_q[i]))
            l_ref[i] = alpha * l_ref[i] + jnp.sum(prob, axis=-1, keepdims=True)
            acc_ref[i] = alpha * acc_ref[i] + jnp.dot(prob.astype(BF16), v2, preferred_element_type=F32)
            m_ref[i] = m_new

    def body(j, carry):
        block(j, False)
        return carry

    lax.fori_loop(0, qi, body, 0)
    block(qi, True)
    out = jnp.where(first, acc_ref[0] / l_ref[0], acc_ref[1] / l_ref[1])
    o_ref[0] = out.astype(BF16)


def _fox_attention(qkv3, ctok, ct):
    b, s_len, _ = qkv3.shape
    q_spec, k_spec, v_spec, o_spec = _attn_specs(s_len)
    return pl.pallas_call(
        _fox_kernel,
        grid=(b, N_PAIRS, s_len // TQ),
        in_specs=[q_spec, k_spec, v_spec,
                  pl.BlockSpec((1, TQ, LANES), lambda b, p, i: (b, i, 0)),
                  pl.BlockSpec((1, SUBLANES, s_len), lambda b, p, i: (b, 0, 0))],
        out_specs=o_spec,
        out_shape=jax.ShapeDtypeStruct((b, s_len, D_ATT), BF16),
        scratch_shapes=[pltpu.VMEM((HEADS_PER_STEP, TQ, 1), F32),
                        pltpu.VMEM((HEADS_PER_STEP, TQ, 1), F32),
                        pltpu.VMEM((HEADS_PER_STEP, TQ, LANES), F32)],
        compiler_params=pltpu.CompilerParams(dimension_semantics=("arbitrary",) * 3),
        name="fox_attn",
    )(qkv3, qkv3, qkv3, ctok, ct)


def _sb_kernel(q_ref, k_ref, v_ref, o_ref, carry_ref, acc_ref):
    qi = pl.program_id(2)
    first, qh = _split_heads(q_ref)
    row = lax.broadcasted_iota(jnp.int32, (TQ, TK), 0)
    col = lax.broadcasted_iota(jnp.int32, (TQ, TK), 1)
    trow = lax.broadcasted_iota(jnp.int32, (TK, TK), 0)
    tcol = lax.broadcasted_iota(jnp.int32, (TK, TK), 1)
    tri = (trow >= tcol).astype(BF16)

    carry_ref[...] = jnp.zeros(carry_ref.shape, F32)
    acc_ref[...] = jnp.zeros(acc_ref.shape, F32)

    def block(j, masked):
        keys = pl.ds(pl.multiple_of(j * TK, TK), TK)
        k2 = k_ref[0, keys, :]
        v2 = v_ref[0, keys, :]
        for i in range(HEADS_PER_STEP):
            z = lax.dot_general(qh[i], k2, _NT, preferred_element_type=F32)
            if masked:
                z = jnp.where(col < row, z, NEG_BIG)
            sp = jnp.maximum(z, 0.0) + jnp.log(1.0 + jnp.exp(-jnp.abs(z)))
            hi = sp.astype(BF16)
            lo = (sp - hi.astype(F32)).astype(BF16)
            csum = (jnp.dot(hi, tri, preferred_element_type=F32)
                    + jnp.dot(lo, tri, preferred_element_type=F32))
            w = jnp.exp(z - csum - carry_ref[i])
            acc_ref[i] += jnp.dot(w.astype(BF16), v2, preferred_element_type=F32)
            carry_ref[i] += csum[:, 0:1]

    block(qi, True)

    def body(t, carry):
        block(qi - 1 - t, False)
        return carry

    lax.fori_loop(0, qi, body, 0)
    o_ref[0] = jnp.where(first, acc_ref[0], acc_ref[1]).astype(BF16)


def _sb_attention(qkv3):
    b, s_len, _ = qkv3.shape
    q_spec, k_spec, v_spec, o_spec = _attn_specs(s_len)
    return pl.pallas_call(
        _sb_kernel,
        grid=(b, N_PAIRS, s_len // TQ),
        in_specs=[q_spec, k_spec, v_spec],
        out_specs=o_spec,
        out_shape=jax.ShapeDtypeStruct((b, s_len, D_ATT), BF16),
        scratch_shapes=[pltpu.VMEM((HEADS_PER_STEP, TQ, 1), F32),
                        pltpu.VMEM((HEADS_PER_STEP, TQ, LANES), F32)],
        compiler_params=pltpu.CompilerParams(dimension_semantics=("arbitrary",) * 3),
        name="sb_attn",
    )(qkv3, qkv3, qkv3)


def _post_kernel(x_ref, of_ref, os_ref, gl_ref, p_ref, bg_ref, wbf_ref, wbs_ref, wout_ref,
                 gmlp_ref, wup_ref, wdown_ref, gple_ref, wpg_ref, wple_ref, gfin_ref, out_ref):
    o_fox = jnp.dot(of_ref[...], wbf_ref[...], preferred_element_type=F32)
    o_sb = jnp.dot(os_ref[...], wbs_ref[...], preferred_element_type=F32)
    gate_a = jax.nn.sigmoid(gl_ref[:, :D_MODEL].astype(F32) + bg_ref[0:1, :])
    gate_b = jax.nn.sigmoid(gl_ref[:, D_MODEL:].astype(F32) + bg_ref[1:2, :])
    merged = (gate_a * o_fox + gate_b * o_sb).astype(BF16)
    x1 = x_ref[...] + jnp.dot(merged, wout_ref[...], preferred_element_type=F32)

    h = _rms_scale(x1, gmlp_ref[...]).astype(BF16)
    mlp = jnp.zeros_like(x1)
    for c in range(D_FF // FF_CHUNK):
        cols = slice(c * FF_CHUNK, (c + 1) * FF_CHUNK)
        up = jnp.maximum(jnp.dot(h, wup_ref[:, cols], preferred_element_type=F32), 0.0)
        mlp = mlp + jnp.dot((up * up).astype(BF16), wdown_ref[cols, :], preferred_element_type=F32)
    x2 = x1 + mlp

    h = _rms_scale(x2, gple_ref[...]).astype(BF16)
    gate = jax.nn.sigmoid(jnp.dot(h, wpg_ref[...], preferred_element_type=F32))
    emb = jnp.dot(p_ref[...].astype(BF16), wple_ref[...], preferred_element_type=F32)
    x3 = x2 + gate * emb
    out_ref[...] = _rms_scale(x3, gfin_ref[...])


def _post(x2, o_fox, o_sb, gl, p2, consts):
    t = x2.shape[0]
    tm = TM_POST
    row = lambda n: pl.BlockSpec((tm, n), lambda i: (i, 0))
    w_bytes = sum(int(c.size) * c.dtype.itemsize for c in consts)
    tile_bytes = tm * (4 * D_MODEL + 2 * D_ATT + 2 * D_ATT + 2 * 2 * D_MODEL + 4 * D_PLE + 4 * D_MODEL)
    scratch_bytes = tm * (6 * 4 * D_MODEL + 6 * FF_CHUNK)
    return pl.pallas_call(
        _post_kernel,
        grid=(t // tm,),
        in_specs=[row(D_MODEL), row(D_ATT), row(D_ATT), row(2 * D_MODEL), row(D_PLE)]
                 + [_const_spec(c.shape) for c in consts],
        out_specs=row(D_MODEL),
        out_shape=jax.ShapeDtypeStruct((t, D_MODEL), F32),
        compiler_params=pltpu.CompilerParams(
            dimension_semantics=("arbitrary",),
            vmem_limit_bytes=w_bytes + 2 * tile_bytes + scratch_bytes),
        name="post",
    )(x2, o_fox, o_sb, gl, p2, *consts)


def kernel(x, p, g_mix, w_in, b_forget, b_gate, w_branch_fox, w_branch_sb, w_out,
           g_mlp, w_up, w_down, g_ple, w_ple_gate, w_ple, g_final):
    b, s_len, d = x.shape
    t = b * s_len
    depth = w_in.shape[0]
    assert depth == 1, "the post kernel fuses the final RMSNorm into the (single) layer"
    x2 = x.reshape(t, d)
    for i in range(depth):
        o0, o1, o2 = 3 * D_ATT, 3 * D_ATT + N_HEADS, 6 * D_ATT + N_HEADS
        wi = w_in[i].astype(BF16)
        wa, wf, wb, wg = wi[:, :o0], wi[:, o0:o1], wi[:, o1:o2], wi[:, o2:]
        wf = jnp.pad(wf, ((0, 0), (0, LANES - N_HEADS)))
        b_pad = jnp.pad(b_forget[i], (0, LANES - N_HEADS)).reshape(1, LANES)

        qkv_fox, f_log, qkv_sb, gl = _in_proj(x2, g_mix[i].reshape(1, d), wa, wf, wb, wg)
        ctok, ct = _forget_cumsum(f_log.reshape(b, s_len, LANES), b_pad)
        o_fox = _fox_attention(qkv_fox.reshape(b, s_len, 3 * D_ATT), ctok, ct)
        o_sb = _sb_attention(qkv_sb.reshape(b, s_len, 3 * D_ATT))

        consts = (b_gate[i], w_branch_fox[i].astype(BF16), w_branch_sb[i].astype(BF16),
                  w_out[i].astype(BF16), g_mlp[i].reshape(1, d), w_up[i].astype(BF16),
                  w_down[i].astype(BF16), g_ple[i].reshape(1, d), w_ple_gate[i].astype(BF16),
                  w_ple[i].astype(BF16), g_final.reshape(1, d))
        x2 = _post(x2, o_fox.reshape(t, D_ATT), o_sb.reshape(t, D_ATT), gl,
                   p[i].reshape(t, D_PLE), consts)
    return x2.reshape(b, s_len, d)
```

```python
import jax
import jax.numpy as jnp
from jax import lax
from jax.experimental import pallas as pl
from jax.experimental.pallas import tpu as pltpu

D_MODEL = 1024
HEAD_DIM = 64
N_HEADS = 8
D_ATT = N_HEADS * HEAD_DIM
D_FF = 4 * D_MODEL
D_PLE = 256
EPS = 1e-6
SCALE = HEAD_DIM ** -0.5

LANES = 128
SUBLANES = 8
PAIR_LANES = 2 * HEAD_DIM
HEADS_PER_STEP = 4
PAIRS_PER_STEP = HEADS_PER_STEP // 2
STEP_LANES = HEADS_PER_STEP * HEAD_DIM
NEG_BIG = -1e30

TM_PROJ = 512
TM_POST = 512
FF_CHUNK = 512
TQ = 256
TK = 256
CUM_BLK = 128

BF16 = jnp.bfloat16
F32 = jnp.float32

_NT = (((1,), (1,)), ((), ()))


def _const_spec(shape):
    return pl.BlockSpec(shape, lambda *_: (0,) * len(shape), pipeline_mode=pl.Buffered(1))


def _rms_scale(x, g):
    r = lax.rsqrt(jnp.mean(x * x, axis=-1, keepdims=True) + EPS)
    return x * r * g


def _in_proj_kernel(x_ref, g_ref, wa_ref, wf_ref, wb_ref, wg_ref,
                    qkva_ref, f_ref, qkvb_ref, gl_ref):
    h = _rms_scale(x_ref[...], g_ref[...]).astype(BF16)
    qkva_ref[...] = jnp.dot(h, wa_ref[...], preferred_element_type=F32).astype(BF16)
    f_ref[...] = jnp.dot(h, wf_ref[...], preferred_element_type=F32)
    qkvb_ref[...] = jnp.dot(h, wb_ref[...], preferred_element_type=F32).astype(BF16)
    gl_ref[...] = jnp.dot(h, wg_ref[...], preferred_element_type=F32).astype(BF16)


def _in_proj(x2, g_mix, wa, wf, wb, wg):
    t = x2.shape[0]
    tm = TM_PROJ
    row = lambda n: pl.BlockSpec((tm, n), lambda i: (i, 0))
    w_bytes = 2 * D_MODEL * (wa.shape[1] + wf.shape[1] + wb.shape[1] + wg.shape[1])
    tile_bytes = tm * (4 * D_MODEL + 2 * wa.shape[1] + 4 * wf.shape[1] + 2 * wb.shape[1] + 2 * wg.shape[1])
    scratch_bytes = 4 * tm * wg.shape[1] + 2 * tm * D_MODEL
    return pl.pallas_call(
        _in_proj_kernel,
        grid=(t // tm,),
        in_specs=[row(D_MODEL), _const_spec(g_mix.shape), _const_spec(wa.shape), _const_spec(wf.shape),
                  _const_spec(wb.shape), _const_spec(wg.shape)],
        out_specs=[row(wa.shape[1]), row(wf.shape[1]), row(wb.shape[1]), row(wg.shape[1])],
        out_shape=[jax.ShapeDtypeStruct((t, wa.shape[1]), BF16),
                   jax.ShapeDtypeStruct((t, wf.shape[1]), F32),
                   jax.ShapeDtypeStruct((t, wb.shape[1]), BF16),
                   jax.ShapeDtypeStruct((t, wg.shape[1]), BF16)],
        compiler_params=pltpu.CompilerParams(
            dimension_semantics=("arbitrary",),
            vmem_limit_bytes=w_bytes + 2 * tile_bytes + 2 * scratch_bytes),
        name="in_proj",
    )(x2, g_mix, wa, wf, wb, wg)


def _split3(a):
    a1 = a.astype(BF16)
    r1 = a - a1.astype(F32)
    a2 = r1.astype(BF16)
    a3 = (r1 - a2.astype(F32)).astype(BF16)
    return a1, a2, a3


def _forget_cumsum_kernel(f_ref, b_ref, ctok_ref, ct_ref):
    s_len = f_ref.shape[1]
    row = lax.broadcasted_iota(jnp.int32, (CUM_BLK, CUM_BLK), 0)
    col = lax.broadcasted_iota(jnp.int32, (CUM_BLK, CUM_BLK), 1)
    tri = (col <= row).astype(BF16)
    carry = jnp.zeros((1, LANES), F32)
    for blk in range(s_len // CUM_BLK):
        rows = pl.ds(blk * CUM_BLK, CUM_BLK)
        y = f_ref[0, rows, :] + b_ref[...]
        log_f = -(jnp.maximum(-y, 0.0) + jnp.log(1.0 + jnp.exp(-jnp.abs(y))))
        c = carry
        for part in _split3(log_f):
            c = c + jnp.dot(tri, part, preferred_element_type=F32)
        carry = c[CUM_BLK - 1:CUM_BLK, :]
        ctok_ref[0, rows, :] = c
        ct_ref[0, :, rows] = jnp.transpose(c)[:SUBLANES, :]


def _forget_cumsum(f3, b_pad):
    b, s_len, _ = f3.shape
    return pl.pallas_call(
        _forget_cumsum_kernel,
        grid=(b,),
        in_specs=[pl.BlockSpec((1, s_len, LANES), lambda i: (i, 0, 0)), _const_spec(b_pad.shape)],
        out_specs=[pl.BlockSpec((1, s_len, LANES), lambda i: (i, 0, 0)),
                   pl.BlockSpec((1, SUBLANES, s_len), lambda i: (i, 0, 0))],
        out_shape=[jax.ShapeDtypeStruct((b, s_len, LANES), F32),
                   jax.ShapeDtypeStruct((b, SUBLANES, s_len), F32)],
        compiler_params=pltpu.CompilerParams(dimension_semantics=("arbitrary",)),
        name="forget_cumsum",
    )(f3, b_pad)


def _lane_index(shape):
    rows = lax.broadcasted_iota(jnp.int32, shape, 0)
    lanes = lax.broadcasted_iota(jnp.int32, shape, 1)
    return (rows * shape[1] + lanes) & (shape[1] - 1)


def _stack_masked_queries(q_ref, qs_ref):
    head_of_lane = _lane_index((TQ, STEP_LANES)) // HEAD_DIM
    q = q_ref[0] * jnp.asarray(SCALE, BF16)
    zero = jnp.zeros_like(q)
    for h in range(HEADS_PER_STEP):
        qs_ref[h * TQ:(h + 1) * TQ, :] = jnp.where(head_of_lane == h, q, zero)


def _block_diag_values(v_pair, with_ones):
    second = _lane_index((TK, PAIR_LANES)) // HEAD_DIM
    zero = jnp.zeros_like(v_pair)
    top, bot = jnp.where(second == 0, v_pair, zero), jnp.where(second == 0, zero, v_pair)
    if with_ones:
        top = jnp.concatenate([top, (1 - second).astype(F32).astype(BF16)], axis=1)
        bot = jnp.concatenate([bot, second.astype(F32).astype(BF16)], axis=1)
    return jnp.concatenate([top, bot], axis=0)


def _attn_specs(s_len):
    groups = D_ATT // STEP_LANES
    q_spec = pl.BlockSpec((1, TQ, STEP_LANES), lambda b, g, i: (b, i, g))
    k_spec = pl.BlockSpec((1, s_len, STEP_LANES), lambda b, g, i: (b, 0, groups + g))
    v_spec = pl.BlockSpec((1, s_len, STEP_LANES), lambda b, g, i: (b, 0, 2 * groups + g))
    o_spec = pl.BlockSpec((1, TQ, STEP_LANES), lambda b, g, i: (b, i, g))
    return q_spec, k_spec, v_spec, o_spec


def _head_rows(x, h):
    return x[h * TQ:(h + 1) * TQ]


def _fox_kernel(q_ref, k_ref, v_ref, ctok_ref, ct_ref, o_ref, qs_ref, acc_ref):
    group = pl.program_id(1)
    qi = pl.program_id(2)
    _stack_masked_queries(q_ref, qs_ref)
    lane = lax.broadcasted_iota(jnp.int32, (TQ, LANES), 1)
    sub = lax.broadcasted_iota(jnp.int32, (SUBLANES, TK), 0)
    row = lax.broadcasted_iota(jnp.int32, (TQ, TK), 0)
    col = lax.broadcasted_iota(jnp.int32, (TQ, TK), 1)
    acc_lane = lax.broadcasted_iota(jnp.int32, (TQ, 2 * PAIR_LANES), 1)
    first_of_pair = (acc_lane & HEAD_DIM) == 0
    ctok = ctok_ref[0]
    c_q = [jnp.sum(jnp.where(lane == HEADS_PER_STEP * group + h, ctok, 0.0), axis=-1, keepdims=True)
           for h in range(HEADS_PER_STEP)]

    acc_ref[...] = jnp.zeros(acc_ref.shape, F32)

    def block(j, m_run, masked):
        keys = pl.ds(pl.multiple_of(j * TK, TK), TK)
        k4 = k_ref[0, keys, :]
        v4 = v_ref[0, keys, :]
        c8 = ct_ref[0, :, keys]
        s_all = lax.dot_general(qs_ref[...], k4, _NT, preferred_element_type=F32)
        probs, alphas, m_next = [], [], []
        for h in range(HEADS_PER_STEP):
            c_k = jnp.sum(jnp.where(sub == HEADS_PER_STEP * group + h, c8, 0.0), axis=0, keepdims=True)
            u = _head_rows(s_all, h) - c_k
            if masked:
                u = jnp.where(col <= row, u, NEG_BIG)
            m_new = jnp.maximum(m_run[h], jnp.max(u, axis=-1, keepdims=True) + c_q[h])
            alphas.append(jnp.exp(m_run[h] - m_new))
            probs.append(jnp.exp(u - (m_new - c_q[h])).astype(BF16))
            m_next.append(m_new)
        for p in range(PAIRS_PER_STEP):
            vbd = _block_diag_values(v4[:, p * PAIR_LANES:(p + 1) * PAIR_LANES], with_ones=True)
            pv = jnp.dot(jnp.concatenate([probs[2 * p], probs[2 * p + 1]], axis=1), vbd,
                         preferred_element_type=F32)
            alpha = jnp.where(first_of_pair, alphas[2 * p], alphas[2 * p + 1])
            acc_ref[p] = alpha * acc_ref[p] + pv
        return tuple(m_next)

    m_init = tuple(jnp.full((TQ, 1), NEG_BIG, F32) for _ in range(HEADS_PER_STEP))
    m_run = lax.fori_loop(0, qi, lambda j, m: block(j, m, False), m_init)
    block(qi, m_run, True)
    outs = [acc_ref[p][:, :PAIR_LANES] / acc_ref[p][:, PAIR_LANES:] for p in range(PAIRS_PER_STEP)]
    o_ref[0] = jnp.concatenate(outs, axis=1).astype(BF16)


def _fox_attention(qkv3, ctok, ct):
    b, s_len, _ = qkv3.shape
    q_spec, k_spec, v_spec, o_spec = _attn_specs(s_len)
    return pl.pallas_call(
        _fox_kernel,
        grid=(b, D_ATT // STEP_LANES, s_len // TQ),
        in_specs=[q_spec, k_spec, v_spec,
                  pl.BlockSpec((1, TQ, LANES), lambda b, g, i: (b, i, 0)),
                  pl.BlockSpec((1, SUBLANES, s_len), lambda b, g, i: (b, 0, 0))],
        out_specs=o_spec,
        out_shape=jax.ShapeDtypeStruct((b, s_len, D_ATT), BF16),
        scratch_shapes=[pltpu.VMEM((HEADS_PER_STEP * TQ, STEP_LANES), BF16),
                        pltpu.VMEM((PAIRS_PER_STEP, TQ, 2 * PAIR_LANES), F32)],
        compiler_params=pltpu.CompilerParams(dimension_semantics=("arbitrary",) * 3),
        name="fox_attn",
    )(qkv3, qkv3, qkv3, ctok, ct)


def _sb_kernel(q_ref, k_ref, v_ref, o_ref, qs_ref, acc_ref):
    qi = pl.program_id(2)
    _stack_masked_queries(q_ref, qs_ref)
    row = lax.broadcasted_iota(jnp.int32, (TQ, TK), 0)
    col = lax.broadcasted_iota(jnp.int32, (TQ, TK), 1)
    trow = lax.broadcasted_iota(jnp.int32, (TK, TK), 0)
    tcol = lax.broadcasted_iota(jnp.int32, (TK, TK), 1)
    tri = (trow >= tcol).astype(BF16)

    acc_ref[...] = jnp.zeros(acc_ref.shape, F32)

    def block(j, carry, masked):
        keys = pl.ds(pl.multiple_of(j * TK, TK), TK)
        k4 = k_ref[0, keys, :]
        v4 = v_ref[0, keys, :]
        z_all = lax.dot_general(qs_ref[...], k4, _NT, preferred_element_type=F32)
        zs, his, los = [], [], []
        for h in range(HEADS_PER_STEP):
            z = _head_rows(z_all, h)
            if masked:
                z = jnp.where(col < row, z, NEG_BIG)
            sp = jnp.maximum(z, 0.0) + jnp.log(1.0 + jnp.exp(-jnp.abs(z)))
            hi = sp.astype(BF16)
            zs.append(z)
            his.append(hi)
            los.append((sp - hi.astype(F32)).astype(BF16))
        csum_all = (jnp.dot(jnp.concatenate(his, axis=0), tri, preferred_element_type=F32)
                    + jnp.dot(jnp.concatenate(los, axis=0), tri, preferred_element_type=F32))
        ws, carry_next = [], []
        for h in range(HEADS_PER_STEP):
            csum = _head_rows(csum_all, h)
            ws.append(jnp.exp(zs[h] - csum - carry[h]).astype(BF16))
            carry_next.append(carry[h] + csum[:, 0:1])
        for p in range(PAIRS_PER_STEP):
            vbd = _block_diag_values(v4[:, p * PAIR_LANES:(p + 1) * PAIR_LANES], with_ones=False)
            acc_ref[p] += jnp.dot(jnp.concatenate([ws[2 * p], ws[2 * p + 1]], axis=1), vbd,
                                  preferred_element_type=F32)
        return tuple(carry_next)

    carry = block(qi, tuple(jnp.zeros((TQ, 1), F32) for _ in range(HEADS_PER_STEP)), True)
    lax.fori_loop(0, qi, lambda t, c: block(qi - 1 - t, c, False), carry)
    o_ref[0] = jnp.concatenate([acc_ref[p] for p in range(PAIRS_PER_STEP)], axis=1).astype(BF16)


def _sb_attention(qkv3):
    b, s_len, _ = qkv3.shape
    q_spec, k_spec, v_spec, o_spec = _attn_specs(s_len)
    return pl.pallas_call(
        _sb_kernel,
        grid=(b, D_ATT // STEP_LANES, s_len // TQ),
        in_specs=[q_spec, k_spec, v_spec],
        out_specs=o_spec,
        out_shape=jax.ShapeDtypeStruct((b, s_len, D_ATT), BF16),
        scratch_shapes=[pltpu.VMEM((HEADS_PER_STEP * TQ, STEP_LANES), BF16),
                        pltpu.VMEM((PAIRS_PER_STEP, TQ, PAIR_LANES), F32)],
        compiler_params=pltpu.CompilerParams(dimension_semantics=("arbitrary",) * 3),
        name="sb_attn",
    )(qkv3, qkv3, qkv3)


def _post_kernel(x_ref, of_ref, os_ref, gl_ref, p_ref, bg_ref, wbf_ref, wbs_ref, wout_ref,
                 gmlp_ref, wup_ref, wdown_ref, gple_ref, wpg_ref, wple_ref, gfin_ref, out_ref):
    o_fox = jnp.dot(of_ref[...], wbf_ref[...], preferred_element_type=F32)
    o_sb = jnp.dot(os_ref[...], wbs_ref[...], preferred_element_type=F32)
    gate_a = jax.nn.sigmoid(gl_ref[:, :D_MODEL].astype(F32) + bg_ref[0:1, :])
    gate_b = jax.nn.sigmoid(gl_ref[:, D_MODEL:].astype(F32) + bg_ref[1:2, :])
    merged = (gate_a * o_fox + gate_b * o_sb).astype(BF16)
    x1 = x_ref[...] + jnp.dot(merged, wout_ref[...], preferred_element_type=F32)

    h = _rms_scale(x1, gmlp_ref[...]).astype(BF16)
    mlp = jnp.zeros_like(x1)
    for c in range(D_FF // FF_CHUNK):
        cols = slice(c * FF_CHUNK, (c + 1) * FF_CHUNK)
        up = jnp.maximum(jnp.dot(h, wup_ref[:, cols], preferred_element_type=F32), 0.0)
        mlp = mlp + jnp.dot((up * up).astype(BF16), wdown_ref[cols, :], preferred_element_type=F32)
    x2 = x1 + mlp

    h = _rms_scale(x2, gple_ref[...]).astype(BF16)
    gate = jax.nn.sigmoid(jnp.dot(h, wpg_ref[...], preferred_element_type=F32))
    emb = jnp.dot(p_ref[...].astype(BF16), wple_ref[...], preferred_element_type=F32)
    x3 = x2 + gate * emb
    out_ref[...] = _rms_scale(x3, gfin_ref[...])


def _post(x2, o_fox, o_sb, gl, p2, consts):
    t = x2.shape[0]
    tm = TM_POST
    row = lambda n: pl.BlockSpec((tm, n), lambda i: (i, 0))
    w_bytes = sum(int(c.size) * c.dtype.itemsize for c in consts)
    tile_bytes = tm * (4 * D_MODEL + 2 * D_ATT + 2 * D_ATT + 2 * 2 * D_MODEL + 4 * D_PLE + 4 * D_MODEL)
    scratch_bytes = tm * (6 * 4 * D_MODEL + 6 * FF_CHUNK)
    return pl.pallas_call(
        _post_kernel,
        grid=(t // tm,),
        in_specs=[row(D_MODEL), row(D_ATT), row(D_ATT), row(2 * D_MODEL), row(D_PLE)]
                 + [_const_spec(c.shape) for c in consts],
        out_specs=row(D_MODEL),
        out_shape=jax.ShapeDtypeStruct((t, D_MODEL), F32),
        compiler_params=pltpu.CompilerParams(
            dimension_semantics=("arbitrary",),
            vmem_limit_bytes=w_bytes + 2 * tile_bytes + scratch_bytes),
        name="post",
    )(x2, o_fox, o_sb, gl, p2, *consts)


def kernel(x, p, g_mix, w_in, b_forget, b_gate, w_branch_fox, w_branch_sb, w_out,
           g_mlp, w_up, w_down, g_ple, w_ple_gate, w_ple, g_final):
    b, s_len, d = x.shape
    t = b * s_len
    depth = w_in.shape[0]
    assert depth == 1, "the post kernel fuses the final RMSNorm into the (single) layer"
    x2 = x.reshape(t, d)
    for i in range(depth):
        o0, o1, o2 = 3 * D_ATT, 3 * D_ATT + N_HEADS, 6 * D_ATT + N_HEADS
        wi = w_in[i].astype(BF16)
        wa, wf, wb, wg = wi[:, :o0], wi[:, o0:o1], wi[:, o1:o2], wi[:, o2:]
        wf = jnp.pad(wf, ((0, 0), (0, LANES - N_HEADS)))
        b_pad = jnp.pad(b_forget[i], (0, LANES - N_HEADS)).reshape(1, LANES)

        qkv_fox, f_log, qkv_sb, gl = _in_proj(x2, g_mix[i].reshape(1, d), wa, wf, wb, wg)
        ctok, ct = _forget_cumsum(f_log.reshape(b, s_len, LANES), b_pad)
        o_fox = _fox_attention(qkv_fox.reshape(b, s_len, 3 * D_ATT), ctok, ct)
        o_sb = _sb_attention(qkv_sb.reshape(b, s_len, 3 * D_ATT))

        consts = (b_gate[i], w_branch_fox[i].astype(BF16), w_branch_sb[i].astype(BF16),
                  w_out[i].astype(BF16), g_mlp[i].reshape(1, d), w_up[i].astype(BF16),
                  w_down[i].astype(BF16), g_ple[i].reshape(1, d), w_ple_gate[i].astype(BF16),
                  w_ple[i].astype(BF16), g_final.reshape(1, d))
        x2 = _post(x2, o_fox.reshape(t, D_ATT), o_sb.reshape(t, D_ATT), gl,
                   p[i].reshape(t, D_PLE), consts)
    return x2.reshape(b, s_len, d)
```

```python
import functools

import jax
import jax.numpy as jnp
import numpy as np
from jax import lax
from jax.experimental import pallas as pl
from jax.experimental.pallas import tpu as pltpu

D_MODEL = 1024
HEAD_DIM = 64
N_HEADS = 8
D_ATT = N_HEADS * HEAD_DIM
D_FF = 4 * D_MODEL
D_PLE = 256
EPS = 1e-6
SCALE = HEAD_DIM ** -0.5
LOG2E = 1.4426950408889634

LANES = 128
SUBLANES = 8
PAIR_LANES = 2 * HEAD_DIM
HEADS_PER_STEP = 4
PAIRS_PER_STEP = HEADS_PER_STEP // 2
STEP_LANES = HEADS_PER_STEP * HEAD_DIM
NEG_BIG = -1e30
SB_DEAD_CARRY = 90.0

TM_PROJ = 512
TM_POST = 512
FF_CHUNK = 512
TQ = 256
TK = 256
CUM_BLK = 128

BF16 = jnp.bfloat16
F32 = jnp.float32

_NT = (((1,), (1,)), ((), ()))


def _const_spec(shape):
    return pl.BlockSpec(shape, lambda *_: (0,) * len(shape), pipeline_mode=pl.Buffered(1))


def _rms_scale(x, g):
    r = lax.rsqrt(jnp.mean(x * x, axis=-1, keepdims=True) + EPS)
    return x * r * g


def _in_proj_kernel(x_ref, g_ref, wa_ref, wf_ref, wb_ref, wg_ref,
                    qkva_ref, f_ref, qkvb_ref, gl_ref):
    h = _rms_scale(x_ref[...], g_ref[...]).astype(BF16)
    for w_ref, out_ref, q_scale in ((wa_ref, qkva_ref, SCALE * LOG2E), (wb_ref, qkvb_ref, SCALE)):
        qkv = jnp.dot(h, w_ref[...], preferred_element_type=F32)
        out_ref[:, :D_ATT] = (qkv[:, :D_ATT] * q_scale).astype(BF16)
        out_ref[:, D_ATT:] = qkv[:, D_ATT:].astype(BF16)
    f_ref[...] = jnp.dot(h, wf_ref[...], preferred_element_type=F32)
    gl_ref[...] = jnp.dot(h, wg_ref[...], preferred_element_type=F32).astype(BF16)


def _in_proj(x2, g_mix, wa, wf, wb, wg):
    t = x2.shape[0]
    tm = TM_PROJ
    row = lambda n: pl.BlockSpec((tm, n), lambda i: (i, 0))
    w_bytes = 2 * D_MODEL * (wa.shape[1] + wf.shape[1] + wb.shape[1] + wg.shape[1])
    tile_bytes = tm * (4 * D_MODEL + 2 * wa.shape[1] + 4 * wf.shape[1] + 2 * wb.shape[1] + 2 * wg.shape[1])
    scratch_bytes = 4 * tm * wg.shape[1] + 2 * tm * D_MODEL
    return pl.pallas_call(
        _in_proj_kernel,
        grid=(t // tm,),
        in_specs=[row(D_MODEL), _const_spec(g_mix.shape), _const_spec(wa.shape), _const_spec(wf.shape),
                  _const_spec(wb.shape), _const_spec(wg.shape)],
        out_specs=[row(wa.shape[1]), row(wf.shape[1]), row(wb.shape[1]), row(wg.shape[1])],
        out_shape=[jax.ShapeDtypeStruct((t, wa.shape[1]), BF16),
                   jax.ShapeDtypeStruct((t, wf.shape[1]), F32),
                   jax.ShapeDtypeStruct((t, wb.shape[1]), BF16),
                   jax.ShapeDtypeStruct((t, wg.shape[1]), BF16)],
        compiler_params=pltpu.CompilerParams(
            dimension_semantics=("arbitrary",),
            vmem_limit_bytes=w_bytes + 2 * tile_bytes + 2 * scratch_bytes),
        name="in_proj",
    )(x2, g_mix, wa, wf, wb, wg)


def _split3(a):
    a1 = a.astype(BF16)
    r1 = a - a1.astype(F32)
    a2 = r1.astype(BF16)
    a3 = (r1 - a2.astype(F32)).astype(BF16)
    return a1, a2, a3


def _forget_cumsum_kernel(f_ref, b_ref, ctok_ref, ct_ref):
    s_len = f_ref.shape[1]
    row = lax.broadcasted_iota(jnp.int32, (CUM_BLK, CUM_BLK), 0)
    col = lax.broadcasted_iota(jnp.int32, (CUM_BLK, CUM_BLK), 1)
    tri = (col <= row).astype(BF16)
    carry = jnp.zeros((1, LANES), F32)
    for blk in range(s_len // CUM_BLK):
        rows = pl.ds(blk * CUM_BLK, CUM_BLK)
        y = f_ref[0, rows, :] + b_ref[...]
        log_f = -(jnp.maximum(-y, 0.0) + jnp.log(1.0 + jnp.exp(-jnp.abs(y))))
        c = carry
        for part in _split3(log_f):
            c = c + jnp.dot(tri, part, preferred_element_type=F32)
        carry = c[CUM_BLK - 1:CUM_BLK, :]
        c2 = c * LOG2E
        ctok_ref[0, rows, :] = c2
        ct_ref[0, :, rows] = jnp.transpose(c2)[:SUBLANES, :]


def _forget_cumsum(f3, b_pad):
    b, s_len, _ = f3.shape
    return pl.pallas_call(
        _forget_cumsum_kernel,
        grid=(b,),
        in_specs=[pl.BlockSpec((1, s_len, LANES), lambda i: (i, 0, 0)), _const_spec(b_pad.shape)],
        out_specs=[pl.BlockSpec((1, s_len, LANES), lambda i: (i, 0, 0)),
                   pl.BlockSpec((1, SUBLANES, s_len), lambda i: (i, 0, 0))],
        out_shape=[jax.ShapeDtypeStruct((b, s_len, LANES), F32),
                   jax.ShapeDtypeStruct((b, SUBLANES, s_len), F32)],
        compiler_params=pltpu.CompilerParams(dimension_semantics=("arbitrary",)),
        name="forget_cumsum",
    )(f3, b_pad)


def _attn_constants(strict):
    lane = np.arange(STEP_LANES)[None, :] // HEAD_DIM
    head = np.repeat(np.arange(HEADS_PER_STEP), TQ)[:, None]
    head_mask = (lane == head).astype(np.float32)
    lane = np.arange(PAIR_LANES)[None, :] // HEAD_DIM
    half = np.repeat(np.arange(2), TK)[:, None]
    pair_mask = (lane == half).astype(np.float32)
    q_pos, k_pos = np.arange(TQ)[:, None], np.arange(TK)[None, :]
    allowed = (k_pos < q_pos) if strict else (k_pos <= q_pos)
    causal = np.where(allowed, 0.0, NEG_BIG).astype(np.float32)
    return jnp.asarray(head_mask, BF16), jnp.asarray(pair_mask, BF16), jnp.asarray(causal, F32)


def _stack_masked_queries(q_ref, head_mask_ref, qs_ref):
    qs_ref[...] = jnp.concatenate([q_ref[0]] * HEADS_PER_STEP, axis=0) * head_mask_ref[...]


def _block_diag_values(v_pair, pair_mask, with_ones):
    vbd = jnp.concatenate([v_pair, v_pair], axis=0) * pair_mask
    return jnp.concatenate([vbd, pair_mask], axis=1) if with_ones else vbd


def _attn_specs(s_len):
    groups = D_ATT // STEP_LANES
    q_spec = pl.BlockSpec((1, TQ, STEP_LANES), lambda b, g, i: (b, i, g))
    k_spec = pl.BlockSpec((1, s_len, STEP_LANES), lambda b, g, i: (b, 0, groups + g))
    v_spec = pl.BlockSpec((1, s_len, STEP_LANES), lambda b, g, i: (b, 0, 2 * groups + g))
    o_spec = pl.BlockSpec((1, TQ, STEP_LANES), lambda b, g, i: (b, i, g))
    return q_spec, k_spec, v_spec, o_spec


_HALVES = tuple(slice(i * PAIRS_PER_STEP * TQ, (i + 1) * PAIRS_PER_STEP * TQ) for i in range(2))


def _head_rows(x, h):
    return x[h * TQ:(h + 1) * TQ]


def _key_rows(j):
    return pl.ds(pl.multiple_of(j * TK, TK), TK)


def _neg_abs(x):
    bits = lax.bitcast_convert_type(x, jnp.uint32) | jnp.uint32(0x80000000)
    return lax.bitcast_convert_type(bits, F32)


def _fox_kernel(q_ref, k_ref, v_ref, ctok_ref, ct_ref, head_mask_ref, pair_mask_ref, causal_ref,
                o_ref, qs_ref, s_ref, p_ref, acc_ref):
    group = pl.program_id(1)
    qi = pl.program_id(2)
    _stack_masked_queries(q_ref, head_mask_ref, qs_ref)
    pair_mask = pair_mask_ref[...]
    lane = lax.broadcasted_iota(jnp.int32, (TQ, LANES), 1)
    sub = lax.broadcasted_iota(jnp.int32, (SUBLANES, TK), 0)
    acc_lane = lax.broadcasted_iota(jnp.int32, (TQ, 2 * PAIR_LANES), 1)
    first_of_pair = (acc_lane & HEAD_DIM) == 0
    ctok = ctok_ref[0]
    c_q = [jnp.sum(jnp.where(lane == HEADS_PER_STEP * group + h, ctok, 0.0), axis=-1, keepdims=True)
           for h in range(HEADS_PER_STEP)]
    acc_ref[...] = jnp.zeros(acc_ref.shape, F32)

    def scores(j):
        k4 = k_ref[0, _key_rows(j), :]
        for half in _HALVES:
            s_ref[half, :] = lax.dot_general(qs_ref[half, :], k4, _NT, preferred_element_type=F32)

    def values(j, alphas):
        v4 = v_ref[0, _key_rows(j), :]
        for p in range(PAIRS_PER_STEP):
            vbd = _block_diag_values(v4[:, p * PAIR_LANES:(p + 1) * PAIR_LANES], pair_mask, with_ones=True)
            alpha = jnp.where(first_of_pair, alphas[2 * p], alphas[2 * p + 1])
            acc_ref[p] = alpha * acc_ref[p] + jnp.dot(p_ref[p], vbd, preferred_element_type=F32)

    def softmax_block(j, m_run, masked):
        c8 = ct_ref[0, :, _key_rows(j)]
        m_next, alphas, probs = [], [], []
        for h in range(HEADS_PER_STEP):
            c_k = jnp.sum(jnp.where(sub == HEADS_PER_STEP * group + h, c8, 0.0), axis=0, keepdims=True)
            u = s_ref[h * TQ:(h + 1) * TQ, :] - c_k
            if masked:
                u = u + causal_ref[...]
            m_new = jnp.maximum(m_run[h], jnp.max(u, axis=-1, keepdims=True) + c_q[h])
            alphas.append(jnp.exp2(m_run[h] - m_new))
            probs.append(jnp.exp2(u - (m_new - c_q[h])).astype(BF16))
            m_next.append(m_new)
        for p in range(PAIRS_PER_STEP):
            p_ref[p] = jnp.concatenate([probs[2 * p], probs[2 * p + 1]], axis=1)
        return tuple(m_next), tuple(alphas)

    scores(qi)
    m_init = tuple(jnp.full((TQ, 1), NEG_BIG, F32) for _ in range(HEADS_PER_STEP))
    state = softmax_block(qi, m_init, True)
    scores(jnp.maximum(qi - 1, 0))

    def step(t, state):
        m_run, alphas = state
        values(qi - t + 1, alphas)
        state = softmax_block(qi - t, m_run, False)
        scores(jnp.maximum(qi - t - 1, 0))
        return state

    _, alphas = lax.fori_loop(1, qi + 1, step, state)
    values(0, alphas)
    outs = [acc_ref[p][:, :PAIR_LANES] / acc_ref[p][:, PAIR_LANES:] for p in range(PAIRS_PER_STEP)]
    o_ref[0] = jnp.concatenate(outs, axis=1).astype(BF16)


def _fox_attention(qkv3, ctok, ct):
    b, s_len, _ = qkv3.shape
    consts = _attn_constants(strict=False)
    q_spec, k_spec, v_spec, o_spec = _attn_specs(s_len)
    return pl.pallas_call(
        _fox_kernel,
        grid=(b, D_ATT // STEP_LANES, s_len // TQ),
        in_specs=[q_spec, k_spec, v_spec,
                  pl.BlockSpec((1, TQ, LANES), lambda b, g, i: (b, i, 0)),
                  pl.BlockSpec((1, SUBLANES, s_len), lambda b, g, i: (b, 0, 0))]
                 + [_const_spec(c.shape) for c in consts],
        out_specs=o_spec,
        out_shape=jax.ShapeDtypeStruct((b, s_len, D_ATT), BF16),
        scratch_shapes=[pltpu.VMEM((HEADS_PER_STEP * TQ, STEP_LANES), BF16),
                        pltpu.VMEM((HEADS_PER_STEP * TQ, TK), F32),
                        pltpu.VMEM((PAIRS_PER_STEP, TQ, 2 * TK), BF16),
                        pltpu.VMEM((PAIRS_PER_STEP, TQ, 2 * PAIR_LANES), F32)],
        compiler_params=pltpu.CompilerParams(dimension_semantics=("arbitrary",) * 3),
        name="fox_attn",
    )(qkv3, qkv3, qkv3, ctok, ct, *consts)


def _sb_kernel(q_ref, k_ref, v_ref, head_mask_ref, pair_mask_ref, causal_ref, tri_ref,
               o_ref, qs_ref, z_ref, w_ref, acc_ref):
    qi = pl.program_id(2)
    _stack_masked_queries(q_ref, head_mask_ref, qs_ref)
    pair_mask = pair_mask_ref[...]
    acc_ref[...] = jnp.zeros(acc_ref.shape, F32)

    def scores(j):
        k4 = k_ref[0, _key_rows(j), :]
        for half in _HALVES:
            z_ref[half, :] = lax.dot_general(qs_ref[half, :], k4, _NT, preferred_element_type=F32)

    def values(j):
        v4 = v_ref[0, _key_rows(j), :]
        for p in range(PAIRS_PER_STEP):
            vbd = _block_diag_values(v4[:, p * PAIR_LANES:(p + 1) * PAIR_LANES], pair_mask, with_ones=False)
            acc_ref[p] += jnp.dot(w_ref[p], vbd, preferred_element_type=F32)

    def weights(carry, masked):
        log_betas, sps = [], []
        for h in range(HEADS_PER_STEP):
            z = z_ref[h * TQ:(h + 1) * TQ, :]
            if masked:
                z = z + causal_ref[...]
            log_beta = jnp.minimum(z, 0.0) - jnp.log(1.0 + jnp.exp(_neg_abs(z)))
            log_betas.append(log_beta)
            sps.append((z - log_beta).astype(BF16))
        csums = [jnp.dot(jnp.concatenate(sps[2 * p:2 * p + 2], axis=0), tri_ref[...],
                         preferred_element_type=F32) for p in range(PAIRS_PER_STEP)]
        ws, carry_next = [], []
        for h in range(HEADS_PER_STEP):
            csum = _head_rows(csums[h // 2], h % 2)
            ws.append(jnp.exp(log_betas[h] - csum - carry[h]).astype(BF16))
            carry_next.append(carry[h] + (csum[:, 0:1] + sps[h][:, 0:1].astype(F32)))
        for p in range(PAIRS_PER_STEP):
            w_ref[p] = jnp.concatenate([ws[2 * p], ws[2 * p + 1]], axis=1)
        return tuple(carry_next)

    def still_live(carry):
        lowest = functools.reduce(jnp.minimum, carry)
        return jnp.min(lowest) < SB_DEAD_CARRY

    scores(qi)
    carry = weights(tuple(jnp.zeros((TQ, 1), F32) for _ in range(HEADS_PER_STEP)), True)
    scores(jnp.maximum(qi - 1, 0))

    def step(state):
        t, carry, _ = state
        values(qi - t + 1)
        carry = weights(carry, False)
        scores(jnp.maximum(qi - t - 1, 0))
        return t + 1, carry, still_live(carry)

    t_end, _, _ = lax.while_loop(lambda st: (st[0] <= qi) & st[2], step,
                                 (jnp.int32(1), carry, still_live(carry)))
    values(qi - t_end + 1)
    o_ref[0] = jnp.concatenate([acc_ref[p] for p in range(PAIRS_PER_STEP)], axis=1).astype(BF16)


def _sb_attention(qkv3):
    b, s_len, _ = qkv3.shape
    tri = jnp.asarray(np.tril(np.ones((TK, TK), np.float32), -1), BF16)
    consts = _attn_constants(strict=True) + (tri,)
    q_spec, k_spec, v_spec, o_spec = _attn_specs(s_len)
    return pl.pallas_call(
        _sb_kernel,
        grid=(b, D_ATT // STEP_LANES, s_len // TQ),
        in_specs=[q_spec, k_spec, v_spec] + [_const_spec(c.shape) for c in consts],
        out_specs=o_spec,
        out_shape=jax.ShapeDtypeStruct((b, s_len, D_ATT), BF16),
        scratch_shapes=[pltpu.VMEM((HEADS_PER_STEP * TQ, STEP_LANES), BF16),
                        pltpu.VMEM((HEADS_PER_STEP * TQ, TK), F32),
                        pltpu.VMEM((PAIRS_PER_STEP, TQ, 2 * TK), BF16),
                        pltpu.VMEM((PAIRS_PER_STEP, TQ, PAIR_LANES), F32)],
        compiler_params=pltpu.CompilerParams(dimension_semantics=("arbitrary",) * 3),
        name="sb_attn",
    )(qkv3, qkv3, qkv3, *consts)


def _post_kernel(x_ref, of_ref, os_ref, gl_ref, p_ref, bg_ref, wbf_ref, wbs_ref, wout_ref,
                 gmlp_ref, wup_ref, wdown_ref, gple_ref, wpg_ref, wple_ref, gfin_ref, out_ref):
    o_fox = jnp.dot(of_ref[...], wbf_ref[...], preferred_element_type=F32)
    o_sb = jnp.dot(os_ref[...], wbs_ref[...], preferred_element_type=F32)
    gate_a = jax.nn.sigmoid(gl_ref[:, :D_MODEL].astype(F32) + bg_ref[0:1, :])
    gate_b = jax.nn.sigmoid(gl_ref[:, D_MODEL:].astype(F32) + bg_ref[1:2, :])
    merged = (gate_a * o_fox + gate_b * o_sb).astype(BF16)
    x1 = x_ref[...] + jnp.dot(merged, wout_ref[...], preferred_element_type=F32)

    h = _rms_scale(x1, gmlp_ref[...]).astype(BF16)
    mlp = jnp.zeros_like(x1)
    for c in range(D_FF // FF_CHUNK):
        cols = slice(c * FF_CHUNK, (c + 1) * FF_CHUNK)
        up = jnp.maximum(jnp.dot(h, wup_ref[:, cols], preferred_element_type=F32), 0.0)
        mlp = mlp + jnp.dot((up * up).astype(BF16), wdown_ref[cols, :], preferred_element_type=F32)
    x2 = x1 + mlp

    h = _rms_scale(x2, gple_ref[...]).astype(BF16)
    gate = jax.nn.sigmoid(jnp.dot(h, wpg_ref[...], preferred_element_type=F32))
    emb = jnp.dot(p_ref[...].astype(BF16), wple_ref[...], preferred_element_type=F32)
    x3 = x2 + gate * emb
    out_ref[...] = _rms_scale(x3, gfin_ref[...])


def _post(x2, o_fox, o_sb, gl, p2, consts):
    t = x2.shape[0]
    tm = TM_POST
    row = lambda n: pl.BlockSpec((tm, n), lambda i: (i, 0))
    w_bytes = sum(int(c.size) * c.dtype.itemsize for c in consts)
    tile_bytes = tm * (4 * D_MODEL + 2 * D_ATT + 2 * D_ATT + 2 * 2 * D_MODEL + 4 * D_PLE + 4 * D_MODEL)
    scratch_bytes = tm * (6 * 4 * D_MODEL + 6 * FF_CHUNK)
    return pl.pallas_call(
        _post_kernel,
        grid=(t // tm,),
        in_specs=[row(D_MODEL), row(D_ATT), row(D_ATT), row(2 * D_MODEL), row(D_PLE)]
                 + [_const_spec(c.shape) for c in consts],
        out_specs=row(D_MODEL),
        out_shape=jax.ShapeDtypeStruct((t, D_MODEL), F32),
        compiler_params=pltpu.CompilerParams(
            dimension_semantics=("arbitrary",),
            vmem_limit_bytes=w_bytes + 2 * tile_bytes + scratch_bytes),
        name="post",
    )(x2, o_fox, o_sb, gl, p2, *consts)


def kernel(x, p, g_mix, w_in, b_forget, b_gate, w_branch_fox, w_branch_sb, w_out,
           g_mlp, w_up, w_down, g_ple, w_ple_gate, w_ple, g_final):
    b, s_len, d = x.shape
    t = b * s_len
    depth = w_in.shape[0]
    assert depth == 1, "the post kernel fuses the final RMSNorm into the (single) layer"
    x2 = x.reshape(t, d)
    for i in range(depth):
        o0, o1, o2 = 3 * D_ATT, 3 * D_ATT + N_HEADS, 6 * D_ATT + N_HEADS
        wi = w_in[i].astype(BF16)
        wa, wf, wb, wg = wi[:, :o0], wi[:, o0:o1], wi[:, o1:o2], wi[:, o2:]
        wf = jnp.pad(wf, ((0, 0), (0, LANES - N_HEADS)))
        b_pad = jnp.pad(b_forget[i], (0, LANES - N_HEADS)).reshape(1, LANES)

        qkv_fox, f_log, qkv_sb, gl = _in_proj(x2, g_mix[i].reshape(1, d), wa, wf, wb, wg)
        ctok, ct = _forget_cumsum(f_log.reshape(b, s_len, LANES), b_pad)
        o_fox = _fox_attention(qkv_fox.reshape(b, s_len, 3 * D_ATT), ctok, ct)
        o_sb = _sb_attention(qkv_sb.reshape(b, s_len, 3 * D_ATT))

        consts = (b_gate[i], w_branch_fox[i].astype(BF16), w_branch_sb[i].astype(BF16),
                  w_out[i].astype(BF16), g_mlp[i].reshape(1, d), w_up[i].astype(BF16),
                  w_down[i].astype(BF16), g_ple[i].reshape(1, d), w_ple_gate[i].astype(BF16),
                  w_ple[i].astype(BF16), g_final.reshape(1, d))
        x2 = _post(x2, o_fox.reshape(t, D_ATT), o_sb.reshape(t, D_ATT), gl,
                   p[i].reshape(t, D_PLE), consts)
    return x2.reshape(b, s_len, d)
```

```python
import functools

import jax
import jax.numpy as jnp
import numpy as np
from jax import lax
from jax.experimental import pallas as pl
from jax.experimental.pallas import tpu as pltpu

D_MODEL = 1024
HEAD_DIM = 64
N_HEADS = 8
D_ATT = N_HEADS * HEAD_DIM
D_FF = 4 * D_MODEL
D_PLE = 256
EPS = 1e-6
SCALE = HEAD_DIM ** -0.5
LOG2E = 1.4426950408889634

LANES = 128
SUBLANES = 8
PAIR_LANES = 2 * HEAD_DIM
HEADS_PER_STEP = 4
PAIRS_PER_STEP = HEADS_PER_STEP // 2
STEP_LANES = HEADS_PER_STEP * HEAD_DIM
NEG_BIG = -1e30
SB_DEAD_CARRY = 90.0

TM_PROJ = 512
TM_POST = 512
FF_CHUNK = 512
TQ = 256
TK = 256
CUM_BLK = 128

BF16 = jnp.bfloat16
F32 = jnp.float32

_NT = (((1,), (1,)), ((), ()))


def _const_spec(shape):
    return pl.BlockSpec(shape, lambda *_: (0,) * len(shape), pipeline_mode=pl.Buffered(1))


def _rms_scale(x, g):
    r = lax.rsqrt(jnp.mean(x * x, axis=-1, keepdims=True) + EPS)
    return x * r * g


def _in_proj_kernel(x_ref, g_ref, wa_ref, wf_ref, wb_ref, wg_ref,
                    qkva_ref, f_ref, qkvb_ref, gl_ref):
    h = _rms_scale(x_ref[...], g_ref[...]).astype(BF16)
    for w_ref, out_ref, q_scale in ((wa_ref, qkva_ref, SCALE * LOG2E), (wb_ref, qkvb_ref, SCALE)):
        qkv = jnp.dot(h, w_ref[...], preferred_element_type=F32)
        out_ref[:, :D_ATT] = (qkv[:, :D_ATT] * q_scale).astype(BF16)
        out_ref[:, D_ATT:] = qkv[:, D_ATT:].astype(BF16)
    f_ref[...] = jnp.dot(h, wf_ref[...], preferred_element_type=F32)
    gl_ref[...] = jnp.dot(h, wg_ref[...], preferred_element_type=F32).astype(BF16)


def _in_proj(x2, g_mix, wa, wf, wb, wg):
    t = x2.shape[0]
    tm = TM_PROJ
    row = lambda n: pl.BlockSpec((tm, n), lambda i: (i, 0))
    w_bytes = 2 * D_MODEL * (wa.shape[1] + wf.shape[1] + wb.shape[1] + wg.shape[1])
    tile_bytes = tm * (4 * D_MODEL + 2 * wa.shape[1] + 4 * wf.shape[1] + 2 * wb.shape[1] + 2 * wg.shape[1])
    scratch_bytes = 4 * tm * wg.shape[1] + 2 * tm * D_MODEL
    return pl.pallas_call(
        _in_proj_kernel,
        grid=(t // tm,),
        in_specs=[row(D_MODEL), _const_spec(g_mix.shape), _const_spec(wa.shape), _const_spec(wf.shape),
                  _const_spec(wb.shape), _const_spec(wg.shape)],
        out_specs=[row(wa.shape[1]), row(wf.shape[1]), row(wb.shape[1]), row(wg.shape[1])],
        out_shape=[jax.ShapeDtypeStruct((t, wa.shape[1]), BF16),
                   jax.ShapeDtypeStruct((t, wf.shape[1]), F32),
                   jax.ShapeDtypeStruct((t, wb.shape[1]), BF16),
                   jax.ShapeDtypeStruct((t, wg.shape[1]), BF16)],
        compiler_params=pltpu.CompilerParams(
            dimension_semantics=("arbitrary",),
            vmem_limit_bytes=w_bytes + 2 * tile_bytes + 2 * scratch_bytes),
        name="in_proj",
    )(x2, g_mix, wa, wf, wb, wg)


def _split3(a):
    a1 = a.astype(BF16)
    r1 = a - a1.astype(F32)
    a2 = r1.astype(BF16)
    a3 = (r1 - a2.astype(F32)).astype(BF16)
    return a1, a2, a3


def _forget_cumsum_kernel(f_ref, b_ref, ctok_ref, ct_ref):
    s_len = f_ref.shape[1]
    row = lax.broadcasted_iota(jnp.int32, (CUM_BLK, CUM_BLK), 0)
    col = lax.broadcasted_iota(jnp.int32, (CUM_BLK, CUM_BLK), 1)
    tri = (col <= row).astype(BF16)
    carry = jnp.zeros((1, LANES), F32)
    for blk in range(s_len // CUM_BLK):
        rows = pl.ds(blk * CUM_BLK, CUM_BLK)
        y = f_ref[0, rows, :] + b_ref[...]
        log_f = -(jnp.maximum(-y, 0.0) + jnp.log(1.0 + jnp.exp(-jnp.abs(y))))
        c = carry
        for part in _split3(log_f):
            c = c + jnp.dot(tri, part, preferred_element_type=F32)
        carry = c[CUM_BLK - 1:CUM_BLK, :]
        c2 = c * LOG2E
        ctok_ref[0, rows, :] = c2
        ct_ref[0, :, rows] = jnp.transpose(c2)[:SUBLANES, :]


def _forget_cumsum(f3, b_pad):
    b, s_len, _ = f3.shape
    return pl.pallas_call(
        _forget_cumsum_kernel,
        grid=(b,),
        in_specs=[pl.BlockSpec((1, s_len, LANES), lambda i: (i, 0, 0)), _const_spec(b_pad.shape)],
        out_specs=[pl.BlockSpec((1, s_len, LANES), lambda i: (i, 0, 0)),
                   pl.BlockSpec((1, SUBLANES, s_len), lambda i: (i, 0, 0))],
        out_shape=[jax.ShapeDtypeStruct((b, s_len, LANES), F32),
                   jax.ShapeDtypeStruct((b, SUBLANES, s_len), F32)],
        compiler_params=pltpu.CompilerParams(dimension_semantics=("arbitrary",)),
        name="forget_cumsum",
    )(f3, b_pad)


def _attn_constants(strict):
    lane = np.arange(STEP_LANES)[None, :] // HEAD_DIM
    head = np.repeat(np.arange(HEADS_PER_STEP), TQ)[:, None]
    head_mask = (lane == head).astype(np.float32)
    lane = np.arange(PAIR_LANES)[None, :] // HEAD_DIM
    half = np.repeat(np.arange(2), TK)[:, None]
    pair_mask = (lane == half).astype(np.float32)
    q_pos, k_pos = np.arange(TQ)[:, None], np.arange(TK)[None, :]
    allowed = (k_pos < q_pos) if strict else (k_pos <= q_pos)
    causal = np.where(allowed, 0.0, NEG_BIG).astype(np.float32)
    return jnp.asarray(head_mask, BF16), jnp.asarray(pair_mask, BF16), jnp.asarray(causal, F32)


def _stack_masked_queries(q_ref, head_mask_ref, qs_ref):
    qs_ref[...] = jnp.concatenate([q_ref[0]] * HEADS_PER_STEP, axis=0) * head_mask_ref[...]


def _block_diag_values(v_pair, pair_mask, with_ones):
    vbd = jnp.concatenate([v_pair, v_pair], axis=0) * pair_mask
    return jnp.concatenate([vbd, pair_mask], axis=1) if with_ones else vbd


def _attn_specs(s_len):
    groups = D_ATT // STEP_LANES
    q_spec = pl.BlockSpec((1, TQ, STEP_LANES), lambda b, g, i: (b, i, g))
    k_spec = pl.BlockSpec((1, s_len, STEP_LANES), lambda b, g, i: (b, 0, groups + g))
    v_spec = pl.BlockSpec((1, s_len, STEP_LANES), lambda b, g, i: (b, 0, 2 * groups + g))
    o_spec = pl.BlockSpec((1, TQ, STEP_LANES), lambda b, g, i: (b, i, g))
    return q_spec, k_spec, v_spec, o_spec


_HALVES = tuple(slice(i * PAIRS_PER_STEP * TQ, (i + 1) * PAIRS_PER_STEP * TQ) for i in range(2))


def _head_rows(x, h):
    return x[h * TQ:(h + 1) * TQ]


def _key_rows(j):
    return pl.ds(pl.multiple_of(j * TK, TK), TK)


def _fox_kernel(q_ref, k_ref, v_ref, ctok_ref, ct_ref, head_mask_ref, pair_mask_ref, causal_ref,
                o_ref, qs_ref, s_ref, p_ref, acc_ref):
    group = pl.program_id(1)
    qi = pl.program_id(2)
    _stack_masked_queries(q_ref, head_mask_ref, qs_ref)
    pair_mask = pair_mask_ref[...]
    lane = lax.broadcasted_iota(jnp.int32, (TQ, LANES), 1)
    sub = lax.broadcasted_iota(jnp.int32, (SUBLANES, TK), 0)
    acc_lane = lax.broadcasted_iota(jnp.int32, (TQ, 2 * PAIR_LANES), 1)
    first_of_pair = (acc_lane & HEAD_DIM) == 0
    ctok = ctok_ref[0]
    c_q = [jnp.sum(jnp.where(lane == HEADS_PER_STEP * group + h, ctok, 0.0), axis=-1, keepdims=True)
           for h in range(HEADS_PER_STEP)]
    acc_ref[...] = jnp.zeros(acc_ref.shape, F32)

    def scores(j):
        k4 = k_ref[0, _key_rows(j), :]
        for half in _HALVES:
            s_ref[half, :] = lax.dot_general(qs_ref[half, :], k4, _NT, preferred_element_type=F32)

    def values(j, alphas):
        v4 = v_ref[0, _key_rows(j), :]
        for p in range(PAIRS_PER_STEP):
            vbd = _block_diag_values(v4[:, p * PAIR_LANES:(p + 1) * PAIR_LANES], pair_mask, with_ones=True)
            alpha = jnp.where(first_of_pair, alphas[2 * p], alphas[2 * p + 1])
            acc_ref[p] = alpha * acc_ref[p] + jnp.dot(p_ref[p], vbd, preferred_element_type=F32)

    def softmax_block(j, m_run, masked):
        c8 = ct_ref[0, :, _key_rows(j)]
        m_next, alphas, probs = [], [], []
        for h in range(HEADS_PER_STEP):
            c_k = jnp.sum(jnp.where(sub == HEADS_PER_STEP * group + h, c8, 0.0), axis=0, keepdims=True)
            u = s_ref[h * TQ:(h + 1) * TQ, :] - c_k
            if masked:
                u = u + causal_ref[...]
            m_new = jnp.maximum(m_run[h], jnp.max(u, axis=-1, keepdims=True) + c_q[h])
            alphas.append(jnp.exp2(m_run[h] - m_new))
            probs.append(jnp.exp2(u - (m_new - c_q[h])).astype(BF16))
            m_next.append(m_new)
        for p in range(PAIRS_PER_STEP):
            p_ref[p] = jnp.concatenate([probs[2 * p], probs[2 * p + 1]], axis=1)
        return tuple(m_next), tuple(alphas)

    scores(qi)
    m_init = tuple(jnp.full((TQ, 1), NEG_BIG, F32) for _ in range(HEADS_PER_STEP))
    state = softmax_block(qi, m_init, True)
    scores(jnp.maximum(qi - 1, 0))

    def step(t, state):
        m_run, alphas = state
        values(qi - t + 1, alphas)
        state = softmax_block(qi - t, m_run, False)
        scores(jnp.maximum(qi - t - 1, 0))
        return state

    _, alphas = lax.fori_loop(1, qi + 1, step, state)
    values(0, alphas)
    outs = [acc_ref[p][:, :PAIR_LANES] / acc_ref[p][:, PAIR_LANES:] for p in range(PAIRS_PER_STEP)]
    o_ref[0] = jnp.concatenate(outs, axis=1).astype(BF16)


def _fox_attention(qkv3, ctok, ct):
    b, s_len, _ = qkv3.shape
    consts = _attn_constants(strict=False)
    q_spec, k_spec, v_spec, o_spec = _attn_specs(s_len)
    return pl.pallas_call(
        _fox_kernel,
        grid=(b, D_ATT // STEP_LANES, s_len // TQ),
        in_specs=[q_spec, k_spec, v_spec,
                  pl.BlockSpec((1, TQ, LANES), lambda b, g, i: (b, i, 0)),
                  pl.BlockSpec((1, SUBLANES, s_len), lambda b, g, i: (b, 0, 0))]
                 + [_const_spec(c.shape) for c in consts],
        out_specs=o_spec,
        out_shape=jax.ShapeDtypeStruct((b, s_len, D_ATT), BF16),
        scratch_shapes=[pltpu.VMEM((HEADS_PER_STEP * TQ, STEP_LANES), BF16),
                        pltpu.VMEM((HEADS_PER_STEP * TQ, TK), F32),
                        pltpu.VMEM((PAIRS_PER_STEP, TQ, 2 * TK), BF16),
                        pltpu.VMEM((PAIRS_PER_STEP, TQ, 2 * PAIR_LANES), F32)],
        compiler_params=pltpu.CompilerParams(dimension_semantics=("arbitrary",) * 3),
        name="fox_attn",
    )(qkv3, qkv3, qkv3, ctok, ct, *consts)


def _sb_kernel(q_ref, k_ref, v_ref, head_mask_ref, pair_mask_ref, causal_ref, tri_ref,
               o_ref, qs_ref, z_ref, w_ref, acc_ref):
    qi = pl.program_id(2)
    _stack_masked_queries(q_ref, head_mask_ref, qs_ref)
    pair_mask = pair_mask_ref[...]
    acc_ref[...] = jnp.zeros(acc_ref.shape, F32)

    def scores(j):
        k4 = k_ref[0, _key_rows(j), :]
        for half in _HALVES:
            z_ref[half, :] = lax.dot_general(qs_ref[half, :], k4, _NT, preferred_element_type=F32)

    def values(j):
        v4 = v_ref[0, _key_rows(j), :]
        for p in range(PAIRS_PER_STEP):
            vbd = _block_diag_values(v4[:, p * PAIR_LANES:(p + 1) * PAIR_LANES], pair_mask, with_ones=False)
            acc_ref[p] += jnp.dot(w_ref[p], vbd, preferred_element_type=F32)

    def weights(carry, masked):
        log_betas, sps = [], []
        for h in range(HEADS_PER_STEP):
            z = z_ref[h * TQ:(h + 1) * TQ, :]
            if masked:
                z = z + causal_ref[...]
            log_beta = jnp.minimum(z, 0.0) - jnp.log(1.0 + jnp.exp2(jnp.abs(z) * -LOG2E))
            log_betas.append(log_beta)
            sps.append((z - log_beta).astype(BF16))
        csums = [jnp.dot(jnp.concatenate(sps[2 * p:2 * p + 2], axis=0), tri_ref[...],
                         preferred_element_type=F32) for p in range(PAIRS_PER_STEP)]
        ws, carry_next = [], []
        for h in range(HEADS_PER_STEP):
            csum = _head_rows(csums[h // 2], h % 2)
            ws.append(jnp.exp(log_betas[h] - csum - carry[h]).astype(BF16))
            carry_next.append(carry[h] + (csum[:, 0:1] + sps[h][:, 0:1].astype(F32)))
        for p in range(PAIRS_PER_STEP):
            w_ref[p] = jnp.concatenate([ws[2 * p], ws[2 * p + 1]], axis=1)
        return tuple(carry_next)

    def still_live(carry):
        lowest = functools.reduce(jnp.minimum, carry)
        return jnp.min(lowest) < SB_DEAD_CARRY

    scores(qi)
    carry = weights(tuple(jnp.zeros((TQ, 1), F32) for _ in range(HEADS_PER_STEP)), True)
    scores(jnp.maximum(qi - 1, 0))

    def step(state):
        t, carry, _ = state
        values(qi - t + 1)
        carry = weights(carry, False)
        scores(jnp.maximum(qi - t - 1, 0))
        return t + 1, carry, still_live(carry)

    t_end, _, _ = lax.while_loop(lambda st: (st[0] <= qi) & st[2], step,
                                 (jnp.int32(1), carry, still_live(carry)))
    values(qi - t_end + 1)
    o_ref[0] = jnp.concatenate([acc_ref[p] for p in range(PAIRS_PER_STEP)], axis=1).astype(BF16)


def _sb_attention(qkv3):
    b, s_len, _ = qkv3.shape
    tri = jnp.asarray(np.tril(np.ones((TK, TK), np.float32), -1), BF16)
    consts = _attn_constants(strict=True) + (tri,)
    q_spec, k_spec, v_spec, o_spec = _attn_specs(s_len)
    return pl.pallas_call(
        _sb_kernel,
        grid=(b, D_ATT // STEP_LANES, s_len // TQ),
        in_specs=[q_spec, k_spec, v_spec] + [_const_spec(c.shape) for c in consts],
        out_specs=o_spec,
        out_shape=jax.ShapeDtypeStruct((b, s_len, D_ATT), BF16),
        scratch_shapes=[pltpu.VMEM((HEADS_PER_STEP * TQ, STEP_LANES), BF16),
                        pltpu.VMEM((HEADS_PER_STEP * TQ, TK), F32),
                        pltpu.VMEM((PAIRS_PER_STEP, TQ, 2 * TK), BF16),
                        pltpu.VMEM((PAIRS_PER_STEP, TQ, PAIR_LANES), F32)],
        compiler_params=pltpu.CompilerParams(dimension_semantics=("arbitrary",) * 3),
        name="sb_attn",
    )(qkv3, qkv3, qkv3, *consts)


def _post_kernel(x_ref, of_ref, os_ref, gl_ref, p_ref, bg_ref, wbf_ref, wbs_ref, wout_ref,
                 gmlp_ref, wup_ref, wdown_ref, gple_ref, wpg_ref, wple_ref, gfin_ref, out_ref):
    o_fox = jnp.dot(of_ref[...], wbf_ref[...], preferred_element_type=F32)
    o_sb = jnp.dot(os_ref[...], wbs_ref[...], preferred_element_type=F32)
    gate_a = jax.nn.sigmoid(gl_ref[:, :D_MODEL].astype(F32) + bg_ref[0:1, :])
    gate_b = jax.nn.sigmoid(gl_ref[:, D_MODEL:].astype(F32) + bg_ref[1:2, :])
    merged = (gate_a * o_fox + gate_b * o_sb).astype(BF16)
    x1 = x_ref[...] + jnp.dot(merged, wout_ref[...], preferred_element_type=F32)

    h = _rms_scale(x1, gmlp_ref[...]).astype(BF16)
    mlp = jnp.zeros_like(x1)
    for c in range(D_FF // FF_CHUNK):
        cols = slice(c * FF_CHUNK, (c + 1) * FF_CHUNK)
        up = jnp.maximum(jnp.dot(h, wup_ref[:, cols], preferred_element_type=F32), 0.0)
        mlp = mlp + jnp.dot((up * up).astype(BF16), wdown_ref[cols, :], preferred_element_type=F32)
    x2 = x1 + mlp

    h = _rms_scale(x2, gple_ref[...]).astype(BF16)
    gate = jax.nn.sigmoid(jnp.dot(h, wpg_ref[...], preferred_element_type=F32))
    emb = jnp.dot(p_ref[...].astype(BF16), wple_ref[...], preferred_element_type=F32)
    x3 = x2 + gate * emb
    out_ref[...] = _rms_scale(x3, gfin_ref[...])


def _post(x2, o_fox, o_sb, gl, p2, consts):
    t = x2.shape[0]
    tm = TM_POST
    row = lambda n: pl.BlockSpec((tm, n), lambda i: (i, 0))
    w_bytes = sum(int(c.size) * c.dtype.itemsize for c in consts)
    tile_bytes = tm * (4 * D_MODEL + 2 * D_ATT + 2 * D_ATT + 2 * 2 * D_MODEL + 4 * D_PLE + 4 * D_MODEL)
    scratch_bytes = tm * (6 * 4 * D_MODEL + 6 * FF_CHUNK)
    return pl.pallas_call(
        _post_kernel,
        grid=(t // tm,),
        in_specs=[row(D_MODEL), row(D_ATT), row(D_ATT), row(2 * D_MODEL), row(D_PLE)]
                 + [_const_spec(c.shape) for c in consts],
        out_specs=row(D_MODEL),
        out_shape=jax.ShapeDtypeStruct((t, D_MODEL), F32),
        compiler_params=pltpu.CompilerParams(
            dimension_semantics=("arbitrary",),
            vmem_limit_bytes=w_bytes + 2 * tile_bytes + scratch_bytes),
        name="post",
    )(x2, o_fox, o_sb, gl, p2, *consts)


def kernel(x, p, g_mix, w_in, b_forget, b_gate, w_branch_fox, w_branch_sb, w_out,
           g_mlp, w_up, w_down, g_ple, w_ple_gate, w_ple, g_final):
    b, s_len, d = x.shape
    t = b * s_len
    depth = w_in.shape[0]
    assert depth == 1, "the post kernel fuses the final RMSNorm into the (single) layer"
    x2 = x.reshape(t, d)
    for i in range(depth):
        o0, o1, o2 = 3 * D_ATT, 3 * D_ATT + N_HEADS, 6 * D_ATT + N_HEADS
        wi = w_in[i].astype(BF16)
        wa, wf, wb, wg = wi[:, :o0], wi[:, o0:o1], wi[:, o1:o2], wi[:, o2:]
        wf = jnp.pad(wf, ((0, 0), (0, LANES - N_HEADS)))
        b_pad = jnp.pad(b_forget[i], (0, LANES - N_HEADS)).reshape(1, LANES)

        qkv_fox, f_log, qkv_sb, gl = _in_proj(x2, g_mix[i].reshape(1, d), wa, wf, wb, wg)
        ctok, ct = _forget_cumsum(f_log.reshape(b, s_len, LANES), b_pad)
        o_fox = _fox_attention(qkv_fox.reshape(b, s_len, 3 * D_ATT), ctok, ct)
        o_sb = _sb_attention(qkv_sb.reshape(b, s_len, 3 * D_ATT))

        consts = (b_gate[i], w_branch_fox[i].astype(BF16), w_branch_sb[i].astype(BF16),
                  w_out[i].astype(BF16), g_mlp[i].reshape(1, d), w_up[i].astype(BF16),
                  w_down[i].astype(BF16), g_ple[i].reshape(1, d), w_ple_gate[i].astype(BF16),
                  w_ple[i].astype(BF16), g_final.reshape(1, d))
        x2 = _post(x2, o_fox.reshape(t, D_ATT), o_sb.reshape(t, D_ATT), gl,
                   p[i].reshape(t, D_PLE), consts)
    return x2.reshape(b, s_len, d)
```

```python
import functools

import jax
import jax.numpy as jnp
import numpy as np
from jax import lax
from jax.experimental import pallas as pl
from jax.experimental.pallas import tpu as pltpu

D_MODEL = 1024
HEAD_DIM = 64
N_HEADS = 8
D_ATT = N_HEADS * HEAD_DIM
D_FF = 4 * D_MODEL
D_PLE = 256
EPS = 1e-6
SCALE = HEAD_DIM ** -0.5
LOG2E = 1.4426950408889634

LANES = 128
SUBLANES = 8
HEADS_PER_STEP = 4
STEP_LANES = HEADS_PER_STEP * HEAD_DIM
NEG_BIG = -1e30
ONES_ROWS = 16
SB_DEAD_CARRY = 90.0

TM_PROJ = 512
TM_POST = 512
FF_CHUNK = 512
TQ = 256
TK = 256
CUM_BLK = 128

BF16 = jnp.bfloat16
F32 = jnp.float32

_NT = (((1,), (1,)), ((), ()))


def _const_spec(shape):
    return pl.BlockSpec(shape, lambda *_: (0,) * len(shape), pipeline_mode=pl.Buffered(1))


def _rms_scale(x, g):
    r = lax.rsqrt(jnp.mean(x * x, axis=-1, keepdims=True) + EPS)
    return x * r * g


def _in_proj_kernel(x_ref, g_ref, wa_ref, wf_ref, wb_ref, wg_ref,
                    qkva_ref, f_ref, qkvb_ref, gl_ref):
    h = _rms_scale(x_ref[...], g_ref[...]).astype(BF16)
    for w_ref, out_ref, q_scale in ((wa_ref, qkva_ref, SCALE * LOG2E), (wb_ref, qkvb_ref, SCALE)):
        qkv = jnp.dot(h, w_ref[...], preferred_element_type=F32)
        out_ref[:, :D_ATT] = (qkv[:, :D_ATT] * q_scale).astype(BF16)
        out_ref[:, D_ATT:] = qkv[:, D_ATT:].astype(BF16)
    f_ref[...] = jnp.dot(h, wf_ref[...], preferred_element_type=F32)
    gl_ref[...] = jnp.dot(h, wg_ref[...], preferred_element_type=F32).astype(BF16)


def _in_proj(x2, g_mix, wa, wf, wb, wg):
    t = x2.shape[0]
    tm = TM_PROJ
    row = lambda n: pl.BlockSpec((tm, n), lambda i: (i, 0))
    w_bytes = 2 * D_MODEL * (wa.shape[1] + wf.shape[1] + wb.shape[1] + wg.shape[1])
    tile_bytes = tm * (4 * D_MODEL + 2 * wa.shape[1] + 4 * wf.shape[1] + 2 * wb.shape[1] + 2 * wg.shape[1])
    scratch_bytes = 4 * tm * wg.shape[1] + 2 * tm * D_MODEL
    return pl.pallas_call(
        _in_proj_kernel,
        grid=(t // tm,),
        in_specs=[row(D_MODEL), _const_spec(g_mix.shape), _const_spec(wa.shape), _const_spec(wf.shape),
                  _const_spec(wb.shape), _const_spec(wg.shape)],
        out_specs=[row(wa.shape[1]), row(wf.shape[1]), row(wb.shape[1]), row(wg.shape[1])],
        out_shape=[jax.ShapeDtypeStruct((t, wa.shape[1]), BF16),
                   jax.ShapeDtypeStruct((t, wf.shape[1]), F32),
                   jax.ShapeDtypeStruct((t, wb.shape[1]), BF16),
                   jax.ShapeDtypeStruct((t, wg.shape[1]), BF16)],
        compiler_params=pltpu.CompilerParams(
            dimension_semantics=("arbitrary",),
            vmem_limit_bytes=w_bytes + 2 * tile_bytes + 2 * scratch_bytes),
        name="in_proj",
    )(x2, g_mix, wa, wf, wb, wg)


def _split3(a):
    a1 = a.astype(BF16)
    r1 = a - a1.astype(F32)
    a2 = r1.astype(BF16)
    a3 = (r1 - a2.astype(F32)).astype(BF16)
    return a1, a2, a3


def _forget_cumsum_kernel(f_ref, b_ref, ctok_ref, ct_ref):
    s_len = f_ref.shape[1]
    row = lax.broadcasted_iota(jnp.int32, (CUM_BLK, CUM_BLK), 0)
    col = lax.broadcasted_iota(jnp.int32, (CUM_BLK, CUM_BLK), 1)
    tri = (col <= row).astype(BF16)
    carry = jnp.zeros((1, LANES), F32)
    for blk in range(s_len // CUM_BLK):
        rows = pl.ds(blk * CUM_BLK, CUM_BLK)
        y = f_ref[0, rows, :] + b_ref[...]
        log_f = -(jnp.maximum(-y, 0.0) + jnp.log(1.0 + jnp.exp(-jnp.abs(y))))
        c = carry
        for part in _split3(log_f):
            c = c + jnp.dot(tri, part, preferred_element_type=F32)
        carry = c[CUM_BLK - 1:CUM_BLK, :]
        c2 = c * LOG2E
        ctok_ref[0, rows, :] = c2
        ct_ref[0, :, rows] = jnp.transpose(c2)[:SUBLANES, :]


def _forget_cumsum(f3, b_pad):
    b, s_len, _ = f3.shape
    return pl.pallas_call(
        _forget_cumsum_kernel,
        grid=(b,),
        in_specs=[pl.BlockSpec((1, s_len, LANES), lambda i: (i, 0, 0)), _const_spec(b_pad.shape)],
        out_specs=[pl.BlockSpec((1, s_len, LANES), lambda i: (i, 0, 0)),
                   pl.BlockSpec((1, SUBLANES, s_len), lambda i: (i, 0, 0))],
        out_shape=[jax.ShapeDtypeStruct((b, s_len, LANES), F32),
                   jax.ShapeDtypeStruct((b, SUBLANES, s_len), F32)],
        compiler_params=pltpu.CompilerParams(dimension_semantics=("arbitrary",)),
        name="forget_cumsum",
    )(f3, b_pad)


def _head_mask():
    lane = np.arange(STEP_LANES)[None, :] // HEAD_DIM
    head = np.repeat(np.arange(HEADS_PER_STEP), TQ)[:, None]
    return jnp.asarray((lane == head).astype(np.float32), BF16)


def _causal_bias_t(strict):
    key_pos, q_pos = np.arange(TK)[:, None], np.arange(TQ)[None, :]
    allowed = (key_pos < q_pos) if strict else (key_pos <= q_pos)
    return jnp.asarray(np.where(allowed, 0.0, NEG_BIG), F32)


def _stack_masked_queries(q_ref, head_mask_ref, qs_ref):
    qs_ref[...] = jnp.concatenate([q_ref[0]] * HEADS_PER_STEP, axis=0) * head_mask_ref[...]


def _transpose_values(v_ref, vt_ref):
    for j in range(v_ref.shape[1] // TK):
        vt_ref[j] = jnp.transpose(v_ref[0, j * TK:(j + 1) * TK, :].astype(F32)).astype(BF16)


def _attn_specs(s_len):
    groups = D_ATT // STEP_LANES
    q_spec = pl.BlockSpec((1, TQ, STEP_LANES), lambda b, g, i: (b, i, g))
    k_spec = pl.BlockSpec((1, s_len, STEP_LANES), lambda b, g, i: (b, 0, groups + g))
    v_spec = pl.BlockSpec((1, s_len, STEP_LANES), lambda b, g, i: (b, 0, 2 * groups + g))
    o_spec = pl.BlockSpec((1, TQ, STEP_LANES), lambda b, g, i: (b, i, g))
    return q_spec, k_spec, v_spec, o_spec


def _key_rows(j):
    return pl.ds(pl.multiple_of(j * TK, TK), TK)


def _fox_kernel(q_ref, k_ref, v_ref, ctok_ref, ct_ref, head_mask_ref, causal_t_ref,
                o_ref, qs_ref, vt_ref, ckb_ref, s_ref, p_ref, acc_ref):
    group = pl.program_id(1)
    qi = pl.program_id(2)
    _stack_masked_queries(q_ref, head_mask_ref, qs_ref)
    sub = lax.broadcasted_iota(jnp.int32, (SUBLANES, TQ), 0)
    ones_rows = jnp.ones((ONES_ROWS, TK), BF16)

    @pl.when(qi == 0)
    def _():
        _transpose_values(v_ref, vt_ref)
        lane = lax.broadcasted_iota(jnp.int32, (TK, LANES), 1)
        for j in range(k_ref.shape[1] // TK):
            rows = slice(j * TK, (j + 1) * TK)
            ctok = ctok_ref[0, rows, :]
            for h in range(HEADS_PER_STEP):
                c_col = jnp.sum(jnp.where(lane == HEADS_PER_STEP * group + h, ctok, 0.0),
                                axis=-1, keepdims=True)
                ckb_ref[h, rows, :] = jnp.broadcast_to(c_col, (TK, LANES))

    c8 = ct_ref[0, :, pl.ds(pl.multiple_of(qi * TQ, TQ), TQ)]
    c_q = [jnp.sum(jnp.where(sub == HEADS_PER_STEP * group + h, c8, 0.0), axis=0, keepdims=True)
           for h in range(HEADS_PER_STEP)]
    acc_ref[...] = jnp.zeros(acc_ref.shape, F32)

    def scores(j):
        k4 = k_ref[0, _key_rows(j), :]
        for h in range(HEADS_PER_STEP):
            s_ref[h] = lax.dot_general(k4, qs_ref[h * TQ:(h + 1) * TQ, :], _NT,
                                       preferred_element_type=F32)

    def values(j, alphas):
        for h in range(HEADS_PER_STEP):
            lhs = jnp.concatenate([vt_ref[j, h * HEAD_DIM:(h + 1) * HEAD_DIM, :], ones_rows], axis=0)
            acc_ref[h] = alphas[h] * acc_ref[h] + jnp.dot(lhs, p_ref[h], preferred_element_type=F32)

    def softmax_block(j, m_run, masked):
        m_next, alphas = [], []
        for h in range(HEADS_PER_STEP):
            c_k = ckb_ref[h, _key_rows(j), :]
            u = s_ref[h] - jnp.concatenate([c_k, c_k], axis=1)
            if masked:
                u = u + causal_t_ref[...]
            m_new = jnp.maximum(m_run[h], jnp.max(u, axis=0, keepdims=True) + c_q[h])
            alphas.append(jnp.exp2(m_run[h] - m_new))
            p_ref[h] = jnp.exp2(u - (m_new - c_q[h])).astype(BF16)
            m_next.append(m_new)
        return tuple(m_next), tuple(alphas)

    scores(qi)
    m_init = tuple(jnp.full((1, TQ), NEG_BIG, F32) for _ in range(HEADS_PER_STEP))
    state = softmax_block(qi, m_init, True)
    scores(jnp.maximum(qi - 1, 0))

    def step(t, state):
        m_run, alphas = state
        values(qi - t + 1, alphas)
        state = softmax_block(qi - t, m_run, False)
        scores(jnp.maximum(qi - t - 1, 0))
        return state

    _, alphas = lax.fori_loop(1, qi + 1, step, state)
    values(0, alphas)
    out_t = jnp.concatenate([acc_ref[h, :HEAD_DIM, :] / acc_ref[h, HEAD_DIM:HEAD_DIM + 1, :]
                             for h in range(HEADS_PER_STEP)], axis=0)
    o_ref[0] = jnp.transpose(out_t).astype(BF16)


def _fox_attention(qkv3, ctok, ct):
    b, s_len, _ = qkv3.shape
    consts = (_head_mask(), _causal_bias_t(strict=False))
    q_spec, k_spec, v_spec, o_spec = _attn_specs(s_len)
    return pl.pallas_call(
        _fox_kernel,
        grid=(b, D_ATT // STEP_LANES, s_len // TQ),
        in_specs=[q_spec, k_spec, v_spec,
                  pl.BlockSpec((1, s_len, LANES), lambda b, g, i: (b, 0, 0)),
                  pl.BlockSpec((1, SUBLANES, s_len), lambda b, g, i: (b, 0, 0))]
                 + [_const_spec(c.shape) for c in consts],
        out_specs=o_spec,
        out_shape=jax.ShapeDtypeStruct((b, s_len, D_ATT), BF16),
        scratch_shapes=[pltpu.VMEM((HEADS_PER_STEP * TQ, STEP_LANES), BF16),
                        pltpu.VMEM((s_len // TK, STEP_LANES, TK), BF16),
                        pltpu.VMEM((HEADS_PER_STEP, s_len, LANES), F32),
                        pltpu.VMEM((HEADS_PER_STEP, TK, TQ), F32),
                        pltpu.VMEM((HEADS_PER_STEP, TK, TQ), BF16),
                        pltpu.VMEM((HEADS_PER_STEP, HEAD_DIM + ONES_ROWS, TQ), F32)],
        compiler_params=pltpu.CompilerParams(dimension_semantics=("arbitrary",) * 3),
        name="fox_attn",
    )(qkv3, qkv3, qkv3, ctok, ct, *consts)


def _sb_kernel(q_ref, k_ref, v_ref, head_mask_ref, causal_t_ref, tri_t_ref,
               o_ref, qs_ref, vt_ref, z_ref, w_ref, acc_ref):
    qi = pl.program_id(2)
    _stack_masked_queries(q_ref, head_mask_ref, qs_ref)

    @pl.when(qi == 0)
    def _():
        _transpose_values(v_ref, vt_ref)

    acc_ref[...] = jnp.zeros(acc_ref.shape, F32)

    def scores(j):
        k4 = k_ref[0, _key_rows(j), :]
        for h in range(HEADS_PER_STEP):
            z_ref[h] = lax.dot_general(k4, qs_ref[h * TQ:(h + 1) * TQ, :], _NT,
                                       preferred_element_type=F32)

    def values(j):
        for h in range(HEADS_PER_STEP):
            acc_ref[h] += jnp.dot(vt_ref[j, h * HEAD_DIM:(h + 1) * HEAD_DIM, :], w_ref[h],
                                  preferred_element_type=F32)

    def weights(carry, masked):
        carry_next = []
        for h in range(HEADS_PER_STEP):
            z = z_ref[h]
            if masked:
                z = z + causal_t_ref[...]
            log_beta = jnp.minimum(z, 0.0) - jnp.log(1.0 + jnp.exp2(jnp.abs(z) * -LOG2E))
            sp = (z - log_beta).astype(BF16)
            csum = jnp.dot(tri_t_ref[...], sp, preferred_element_type=F32)
            w_ref[h] = jnp.exp(log_beta - csum - carry[h]).astype(BF16)
            carry_next.append(carry[h] + (csum[0:1, :] + sp[0:1, :].astype(F32)))
        return tuple(carry_next)

    def still_live(carry):
        lowest = functools.reduce(jnp.minimum, carry)
        return jnp.min(lowest) < SB_DEAD_CARRY

    scores(qi)
    carry = weights(tuple(jnp.zeros((1, TQ), F32) for _ in range(HEADS_PER_STEP)), True)
    scores(jnp.maximum(qi - 1, 0))

    def step(state):
        t, carry, _ = state
        values(qi - t + 1)
        carry = weights(carry, False)
        scores(jnp.maximum(qi - t - 1, 0))
        return t + 1, carry, still_live(carry)

    t_end, _, _ = lax.while_loop(lambda st: (st[0] <= qi) & st[2], step,
                                 (jnp.int32(1), carry, still_live(carry)))
    values(qi - t_end + 1)
    out_t = jnp.concatenate([acc_ref[h] for h in range(HEADS_PER_STEP)], axis=0)
    o_ref[0] = jnp.transpose(out_t).astype(BF16)


def _sb_attention(qkv3):
    b, s_len, _ = qkv3.shape
    tri_t = jnp.asarray(np.triu(np.ones((TK, TK), np.float32), 1), BF16)
    consts = (_head_mask(), _causal_bias_t(strict=True), tri_t)
    q_spec, k_spec, v_spec, o_spec = _attn_specs(s_len)
    return pl.pallas_call(
        _sb_kernel,
        grid=(b, D_ATT // STEP_LANES, s_len // TQ),
        in_specs=[q_spec, k_spec, v_spec] + [_const_spec(c.shape) for c in consts],
        out_specs=o_spec,
        out_shape=jax.ShapeDtypeStruct((b, s_len, D_ATT), BF16),
        scratch_shapes=[pltpu.VMEM((HEADS_PER_STEP * TQ, STEP_LANES), BF16),
                        pltpu.VMEM((s_len // TK, STEP_LANES, TK), BF16),
                        pltpu.VMEM((HEADS_PER_STEP, TK, TQ), F32),
                        pltpu.VMEM((HEADS_PER_STEP, TK, TQ), BF16),
                        pltpu.VMEM((HEADS_PER_STEP, HEAD_DIM, TQ), F32)],
        compiler_params=pltpu.CompilerParams(dimension_semantics=("arbitrary",) * 3),
        name="sb_attn",
    )(qkv3, qkv3, qkv3, *consts)


def _post_kernel(x_ref, of_ref, os_ref, gl_ref, p_ref, bg_ref, wbf_ref, wbs_ref, wout_ref,
                 gmlp_ref, wup_ref, wdown_ref, gple_ref, wpg_ref, wple_ref, gfin_ref, out_ref):
    o_fox = jnp.dot(of_ref[...], wbf_ref[...], preferred_element_type=F32)
    o_sb = jnp.dot(os_ref[...], wbs_ref[...], preferred_element_type=F32)
    gate_a = jax.nn.sigmoid(gl_ref[:, :D_MODEL].astype(F32) + bg_ref[0:1, :])
    gate_b = jax.nn.sigmoid(gl_ref[:, D_MODEL:].astype(F32) + bg_ref[1:2, :])
    merged = (gate_a * o_fox + gate_b * o_sb).astype(BF16)
    x1 = x_ref[...] + jnp.dot(merged, wout_ref[...], preferred_element_type=F32)

    h = _rms_scale(x1, gmlp_ref[...]).astype(BF16)
    mlp = jnp.zeros_like(x1)
    for c in range(D_FF // FF_CHUNK):
        cols = slice(c * FF_CHUNK, (c + 1) * FF_CHUNK)
        up = jnp.maximum(jnp.dot(h, wup_ref[:, cols], preferred_element_type=F32), 0.0)
        mlp = mlp + jnp.dot((up * up).astype(BF16), wdown_ref[cols, :], preferred_element_type=F32)
    x2 = x1 + mlp

    h = _rms_scale(x2, gple_ref[...]).astype(BF16)
    gate = jax.nn.sigmoid(jnp.dot(h, wpg_ref[...], preferred_element_type=F32))
    emb = jnp.dot(p_ref[...].astype(BF16), wple_ref[...], preferred_element_type=F32)
    x3 = x2 + gate * emb
    out_ref[...] = _rms_scale(x3, gfin_ref[...])


def _post(x2, o_fox, o_sb, gl, p2, consts):
    t = x2.shape[0]
    tm = TM_POST
    row = lambda n: pl.BlockSpec((tm, n), lambda i: (i, 0))
    w_bytes = sum(int(c.size) * c.dtype.itemsize for c in consts)
    tile_bytes = tm * (4 * D_MODEL + 2 * D_ATT + 2 * D_ATT + 2 * 2 * D_MODEL + 4 * D_PLE + 4 * D_MODEL)
    scratch_bytes = tm * (6 * 4 * D_MODEL + 6 * FF_CHUNK)
    return pl.pallas_call(
        _post_kernel,
        grid=(t // tm,),
        in_specs=[row(D_MODEL), row(D_ATT), row(D_ATT), row(2 * D_MODEL), row(D_PLE)]
                 + [_const_spec(c.shape) for c in consts],
        out_specs=row(D_MODEL),
        out_shape=jax.ShapeDtypeStruct((t, D_MODEL), F32),
        compiler_params=pltpu.CompilerParams(
            dimension_semantics=("arbitrary",),
            vmem_limit_bytes=w_bytes + 2 * tile_bytes + scratch_bytes),
        name="post",
    )(x2, o_fox, o_sb, gl, p2, *consts)


def kernel(x, p, g_mix, w_in, b_forget, b_gate, w_branch_fox, w_branch_sb, w_out,
           g_mlp, w_up, w_down, g_ple, w_ple_gate, w_ple, g_final):
    b, s_len, d = x.shape
    t = b * s_len
    depth = w_in.shape[0]
    assert depth == 1, "the post kernel fuses the final RMSNorm into the (single) layer"
    x2 = x.reshape(t, d)
    for i in range(depth):
        o0, o1, o2 = 3 * D_ATT, 3 * D_ATT + N_HEADS, 6 * D_ATT + N_HEADS
        wa, wf, wb, wg = (w_in[i][:, lo:hi].astype(BF16) for lo, hi in ((0, o0), (o0, o1), (o1, o2), (o2, None)))
        wf = jnp.pad(wf, ((0, 0), (0, LANES - N_HEADS)))
        b_pad = jnp.pad(b_forget[i], (0, LANES - N_HEADS)).reshape(1, LANES)

        qkv_fox, f_log, qkv_sb, gl = _in_proj(x2, g_mix[i].reshape(1, d), wa, wf, wb, wg)
        ctok, ct = _forget_cumsum(f_log.reshape(b, s_len, LANES), b_pad)
        o_fox = _fox_attention(qkv_fox.reshape(b, s_len, 3 * D_ATT), ctok, ct)
        o_sb = _sb_attention(qkv_sb.reshape(b, s_len, 3 * D_ATT))

        consts = (b_gate[i], w_branch_fox[i].astype(BF16), w_branch_sb[i].astype(BF16),
                  w_out[i].astype(BF16), g_mlp[i].reshape(1, d), w_up[i].astype(BF16),
                  w_down[i].astype(BF16), g_ple[i].reshape(1, d), w_ple_gate[i].astype(BF16),
                  w_ple[i].astype(BF16), g_final.reshape(1, d))
        x2 = _post(x2, o_fox.reshape(t, D_ATT), o_sb.reshape(t, D_ATT), gl,
                   p[i].reshape(t, D_PLE), consts)
    return x2.reshape(b, s_len, d)
```

```python
import functools

import jax
import jax.numpy as jnp
import numpy as np
from jax import lax
from jax.experimental import pallas as pl
from jax.experimental.pallas import tpu as pltpu

D_MODEL = 1024
HEAD_DIM = 64
N_HEADS = 8
D_ATT = N_HEADS * HEAD_DIM
D_FF = 4 * D_MODEL
D_PLE = 256
EPS = 1e-6
SCALE = HEAD_DIM ** -0.5
LOG2E = 1.4426950408889634

LANES = 128
SUBLANES = 8
HEADS_PER_STEP = 4
PAIRS_PER_STEP = HEADS_PER_STEP // 2
PAIR_LANES = 2 * HEAD_DIM
STEP_LANES = HEADS_PER_STEP * HEAD_DIM
NEG_BIG = -1e30
ONES_ROWS = 16
SB_DEAD_CARRY = 90.0

TM_PROJ = 512
TM_POST = 512
FF_CHUNK = 512
TQ = 256
TK = 256
CUM_BLK = 128

BF16 = jnp.bfloat16
F32 = jnp.float32

_NT = (((1,), (1,)), ((), ()))


def _const_spec(shape):
    return pl.BlockSpec(shape, lambda *_: (0,) * len(shape), pipeline_mode=pl.Buffered(1))


def _rms_scale(x, g):
    r = lax.rsqrt(jnp.mean(x * x, axis=-1, keepdims=True) + EPS)
    return x * r * g


def _in_proj_kernel(x_ref, g_ref, wa_ref, wf_ref, wb_ref, wg_ref,
                    qkva_ref, f_ref, qkvb_ref, gl_ref):
    h = _rms_scale(x_ref[...], g_ref[...]).astype(BF16)
    for w_ref, out_ref, q_scale in ((wa_ref, qkva_ref, SCALE * LOG2E), (wb_ref, qkvb_ref, SCALE)):
        qkv = jnp.dot(h, w_ref[...], preferred_element_type=F32)
        out_ref[:, :D_ATT] = (qkv[:, :D_ATT] * q_scale).astype(BF16)
        out_ref[:, D_ATT:] = qkv[:, D_ATT:].astype(BF16)
    f_ref[...] = jnp.dot(h, wf_ref[...], preferred_element_type=F32)
    gl_ref[...] = jnp.dot(h, wg_ref[...], preferred_element_type=F32).astype(BF16)


def _in_proj(x2, g_mix, wa, wf, wb, wg):
    t = x2.shape[0]
    tm = TM_PROJ
    row = lambda n: pl.BlockSpec((tm, n), lambda i: (i, 0))
    w_bytes = 2 * D_MODEL * (wa.shape[1] + wf.shape[1] + wb.shape[1] + wg.shape[1])
    tile_bytes = tm * (4 * D_MODEL + 2 * wa.shape[1] + 4 * wf.shape[1] + 2 * wb.shape[1] + 2 * wg.shape[1])
    scratch_bytes = 4 * tm * wg.shape[1] + 2 * tm * D_MODEL
    return pl.pallas_call(
        _in_proj_kernel,
        grid=(t // tm,),
        in_specs=[row(D_MODEL), _const_spec(g_mix.shape), _const_spec(wa.shape), _const_spec(wf.shape),
                  _const_spec(wb.shape), _const_spec(wg.shape)],
        out_specs=[row(wa.shape[1]), row(wf.shape[1]), row(wb.shape[1]), row(wg.shape[1])],
        out_shape=[jax.ShapeDtypeStruct((t, wa.shape[1]), BF16),
                   jax.ShapeDtypeStruct((t, wf.shape[1]), F32),
                   jax.ShapeDtypeStruct((t, wb.shape[1]), BF16),
                   jax.ShapeDtypeStruct((t, wg.shape[1]), BF16)],
        compiler_params=pltpu.CompilerParams(
            dimension_semantics=("arbitrary",),
            vmem_limit_bytes=w_bytes + 2 * tile_bytes + 2 * scratch_bytes),
        name="in_proj",
    )(x2, g_mix, wa, wf, wb, wg)


def _split3(a):
    a1 = a.astype(BF16)
    r1 = a - a1.astype(F32)
    a2 = r1.astype(BF16)
    a3 = (r1 - a2.astype(F32)).astype(BF16)
    return a1, a2, a3


def _forget_cumsum_kernel(f_ref, b_ref, ctok_ref, ct_ref):
    s_len = f_ref.shape[1]
    row = lax.broadcasted_iota(jnp.int32, (CUM_BLK, CUM_BLK), 0)
    col = lax.broadcasted_iota(jnp.int32, (CUM_BLK, CUM_BLK), 1)
    tri = (col <= row).astype(BF16)
    carry = jnp.zeros((1, LANES), F32)
    for blk in range(s_len // CUM_BLK):
        rows = pl.ds(blk * CUM_BLK, CUM_BLK)
        y = f_ref[0, rows, :] + b_ref[...]
        log_f = -(jnp.maximum(-y, 0.0) + jnp.log(1.0 + jnp.exp(-jnp.abs(y))))
        c = carry
        for part in _split3(log_f):
            c = c + jnp.dot(tri, part, preferred_element_type=F32)
        carry = c[CUM_BLK - 1:CUM_BLK, :]
        c2 = c * LOG2E
        ctok_ref[0, rows, :] = c2
        ct_ref[0, :, rows] = jnp.transpose(c2)[:SUBLANES, :]


def _forget_cumsum(f3, b_pad):
    b, s_len, _ = f3.shape
    return pl.pallas_call(
        _forget_cumsum_kernel,
        grid=(b,),
        in_specs=[pl.BlockSpec((1, s_len, LANES), lambda i: (i, 0, 0)), _const_spec(b_pad.shape)],
        out_specs=[pl.BlockSpec((1, s_len, LANES), lambda i: (i, 0, 0)),
                   pl.BlockSpec((1, SUBLANES, s_len), lambda i: (i, 0, 0))],
        out_shape=[jax.ShapeDtypeStruct((b, s_len, LANES), F32),
                   jax.ShapeDtypeStruct((b, SUBLANES, s_len), F32)],
        compiler_params=pltpu.CompilerParams(dimension_semantics=("arbitrary",)),
        name="forget_cumsum",
    )(f3, b_pad)


def _head_mask():
    lane = np.arange(STEP_LANES)[None, :] // HEAD_DIM
    head = np.repeat(np.arange(HEADS_PER_STEP), TQ)[:, None]
    return jnp.asarray((lane == head).astype(np.float32), BF16)


def _pair_mask():
    lane = np.arange(PAIR_LANES)[None, :] // HEAD_DIM
    half = np.repeat(np.arange(2), TK)[:, None]
    return jnp.asarray((lane == half).astype(np.float32), BF16)


def _causal_bias(strict, keys_on_rows):
    key_pos, q_pos = np.arange(TK)[:, None], np.arange(TQ)[None, :]
    allowed = (key_pos < q_pos) if strict else (key_pos <= q_pos)
    bias = np.where(allowed, 0.0, NEG_BIG).astype(np.float32)
    return jnp.asarray(bias if keys_on_rows else bias.T, F32)


def _stack_masked_queries(q_ref, head_mask_ref, qs_ref):
    qs_ref[...] = jnp.concatenate([q_ref[0]] * HEADS_PER_STEP, axis=0) * head_mask_ref[...]


def _key_rows(j):
    return pl.ds(pl.multiple_of(j * TK, TK), TK)


_HALVES = tuple(slice(i * PAIRS_PER_STEP * TQ, (i + 1) * PAIRS_PER_STEP * TQ) for i in range(2))


def _fox_stages(group, qi, k_ref, ct_ref, causal_t_ref, qs_ref, vt_ref, ckb_ref, s_ref, p_ref, acc_ref):
    sub = lax.broadcasted_iota(jnp.int32, (SUBLANES, TQ), 0)
    ones_rows = jnp.ones((ONES_ROWS, TK), BF16)
    c8 = ct_ref[0, :, pl.ds(pl.multiple_of(qi * TQ, TQ), TQ)]
    c_q = [jnp.sum(jnp.where(sub == HEADS_PER_STEP * group + h, c8, 0.0), axis=0, keepdims=True)
           for h in range(HEADS_PER_STEP)]

    def scores(j):
        k4 = k_ref[0, _key_rows(j), :]
        for h in range(HEADS_PER_STEP):
            s_ref[h] = lax.dot_general(k4, qs_ref[h * TQ:(h + 1) * TQ, :], _NT,
                                       preferred_element_type=F32)

    def values(j, alphas):
        for h in range(HEADS_PER_STEP):
            lhs = jnp.concatenate([vt_ref[j, h * HEAD_DIM:(h + 1) * HEAD_DIM, :], ones_rows], axis=0)
            acc_ref[h] = alphas[h] * acc_ref[h] + jnp.dot(lhs, p_ref[h], preferred_element_type=F32)

    def softmax_block(j, m_run, masked):
        m_next, alphas = [], []
        for h in range(HEADS_PER_STEP):
            c_k = ckb_ref[h, _key_rows(j), :]
            u = s_ref[h] - jnp.concatenate([c_k, c_k], axis=1)
            if masked:
                u = u + causal_t_ref[...]
            m_new = jnp.maximum(m_run[h], jnp.max(u, axis=0, keepdims=True) + c_q[h])
            alphas.append(jnp.exp2(m_run[h] - m_new))
            p_ref[h] = jnp.exp2(u - (m_new - c_q[h])).astype(BF16)
            m_next.append(m_new)
        return tuple(m_next), tuple(alphas)

    return scores, values, softmax_block


def _sb_stages(k_ref, v_ref, pair_mask_ref, causal_ref, tri_ref, qs_ref, z_ref, w_ref, acc_ref):
    pair_mask = pair_mask_ref[...]

    def scores(j):
        k4 = k_ref[0, _key_rows(j), :]
        for half in _HALVES:
            z_ref[half, :] = lax.dot_general(qs_ref[half, :], k4, _NT, preferred_element_type=F32)

    def values(j):
        v4 = v_ref[0, _key_rows(j), :]
        for p in range(PAIRS_PER_STEP):
            v_pair = v4[:, p * PAIR_LANES:(p + 1) * PAIR_LANES]
            vbd = jnp.concatenate([v_pair, v_pair], axis=0) * pair_mask
            acc_ref[p] += jnp.dot(w_ref[p], vbd, preferred_element_type=F32)

    def weights(carry, masked):
        log_betas, sps = [], []
        for h in range(HEADS_PER_STEP):
            z = z_ref[h * TQ:(h + 1) * TQ, :]
            if masked:
                z = z + causal_ref[...]
            log_beta = jnp.minimum(z, 0.0) - jnp.log(1.0 + jnp.exp2(jnp.abs(z) * -LOG2E))
            log_betas.append(log_beta)
            sps.append((z - log_beta).astype(BF16))
        csums = [jnp.dot(jnp.concatenate(sps[2 * p:2 * p + 2], axis=0), tri_ref[...],
                         preferred_element_type=F32) for p in range(PAIRS_PER_STEP)]
        carry_next = []
        for h in range(HEADS_PER_STEP):
            csum = csums[h // 2][(h % 2) * TQ:(h % 2 + 1) * TQ]
            w_ref[h // 2, :, (h % 2) * TK:(h % 2 + 1) * TK] = (
                jnp.exp(log_betas[h] - csum - carry[h]).astype(BF16))
            carry_next.append(carry[h] + (csum[:, 0:1] + sps[h][:, 0:1].astype(F32)))
        return tuple(carry_next)

    return scores, values, weights


def _attn_kernel(fq_ref, fk_ref, fv_ref, ctok_ref, ct_ref, sq_ref, sk_ref, sv_ref,
                 head_mask_ref, fox_causal_ref, pair_mask_ref, sb_causal_ref, tri_ref,
                 fo_ref, so_ref,
                 fqs_ref, vt_ref, ckb_ref, s_ref, p_ref, facc_ref, sqs_ref, z_ref, w_ref, sacc_ref):
    group = pl.program_id(1)
    qi = pl.program_id(2)

    @pl.when(qi == 0)
    def _():
        lane = lax.broadcasted_iota(jnp.int32, (TK, LANES), 1)
        for j in range(fk_ref.shape[1] // TK):
            rows = slice(j * TK, (j + 1) * TK)
            vt_ref[j] = jnp.transpose(fv_ref[0, rows, :].astype(F32)).astype(BF16)
            ctok = ctok_ref[0, rows, :]
            for h in range(HEADS_PER_STEP):
                c_col = jnp.sum(jnp.where(lane == HEADS_PER_STEP * group + h, ctok, 0.0),
                                axis=-1, keepdims=True)
                ckb_ref[h, rows, :] = jnp.broadcast_to(c_col, (TK, LANES))

    _stack_masked_queries(fq_ref, head_mask_ref, fqs_ref)
    _stack_masked_queries(sq_ref, head_mask_ref, sqs_ref)
    facc_ref[...] = jnp.zeros(facc_ref.shape, F32)
    sacc_ref[...] = jnp.zeros(sacc_ref.shape, F32)
    fox_scores, fox_values, fox_softmax = _fox_stages(
        group, qi, fk_ref, ct_ref, fox_causal_ref, fqs_ref, vt_ref, ckb_ref, s_ref, p_ref, facc_ref)
    sb_scores, sb_values, sb_weights = _sb_stages(
        sk_ref, sv_ref, pair_mask_ref, sb_causal_ref, tri_ref, sqs_ref, z_ref, w_ref, sacc_ref)

    def still_live(carry):
        lowest = functools.reduce(jnp.minimum, carry)
        return jnp.min(lowest) < SB_DEAD_CARRY

    fox_scores(qi)
    sb_scores(qi)
    m_init = tuple(jnp.full((1, TQ), NEG_BIG, F32) for _ in range(HEADS_PER_STEP))
    fox_state = fox_softmax(qi, m_init, True)
    carry = sb_weights(tuple(jnp.zeros((TQ, 1), F32) for _ in range(HEADS_PER_STEP)), True)
    nxt = jnp.maximum(qi - 1, 0)
    fox_scores(nxt)
    sb_scores(nxt)

    def fox_step(t, state):
        m_run, alphas = state
        fox_values(qi - t + 1, alphas)
        state = fox_softmax(qi - t, m_run, False)
        fox_scores(jnp.maximum(qi - t - 1, 0))
        return state

    _, alphas = lax.fori_loop(1, qi + 1, fox_step, fox_state)

    def sb_step(state):
        t, carry, _ = state
        sb_values(qi - t + 1)
        carry = sb_weights(carry, False)
        sb_scores(jnp.maximum(qi - t - 1, 0))
        return t + 1, carry, still_live(carry)

    t_end, _, _ = lax.while_loop(lambda st: (st[0] <= qi) & st[2], sb_step,
                                 (jnp.int32(1), carry, still_live(carry)))

    fox_values(0, alphas)
    sb_values(qi - t_end + 1)
    out_t = jnp.concatenate([facc_ref[h, :HEAD_DIM, :] / facc_ref[h, HEAD_DIM:HEAD_DIM + 1, :]
                             for h in range(HEADS_PER_STEP)], axis=0)
    fo_ref[0] = jnp.transpose(out_t).astype(BF16)
    so_ref[0] = jnp.concatenate([sacc_ref[p] for p in range(PAIRS_PER_STEP)], axis=1).astype(BF16)


def _attention(qkv_fox, qkv_sb, ctok, ct):
    b, s_len, _ = qkv_fox.shape
    groups = D_ATT // STEP_LANES
    q_spec = pl.BlockSpec((1, TQ, STEP_LANES), lambda b, g, i: (b, i, g))
    k_spec = pl.BlockSpec((1, s_len, STEP_LANES), lambda b, g, i: (b, 0, groups + g))
    v_spec = pl.BlockSpec((1, s_len, STEP_LANES), lambda b, g, i: (b, 0, 2 * groups + g))
    o_spec = pl.BlockSpec((1, TQ, STEP_LANES), lambda b, g, i: (b, i, g))
    tri = jnp.asarray(np.tril(np.ones((TK, TK), np.float32), -1), BF16)
    consts = (_head_mask(), _causal_bias(strict=False, keys_on_rows=True), _pair_mask(),
              _causal_bias(strict=True, keys_on_rows=False), tri)
    out = jax.ShapeDtypeStruct((b, s_len, D_ATT), BF16)
    return pl.pallas_call(
        _attn_kernel,
        grid=(b, groups, s_len // TQ),
        in_specs=[q_spec, k_spec, v_spec,
                  pl.BlockSpec((1, s_len, LANES), lambda b, g, i: (b, 0, 0)),
                  pl.BlockSpec((1, SUBLANES, s_len), lambda b, g, i: (b, 0, 0)),
                  q_spec, k_spec, v_spec] + [_const_spec(c.shape) for c in consts],
        out_specs=[o_spec, o_spec],
        out_shape=[out, out],
        scratch_shapes=[pltpu.VMEM((HEADS_PER_STEP * TQ, STEP_LANES), BF16),
                        pltpu.VMEM((s_len // TK, STEP_LANES, TK), BF16),
                        pltpu.VMEM((HEADS_PER_STEP, s_len, LANES), F32),
                        pltpu.VMEM((HEADS_PER_STEP, TK, TQ), F32),
                        pltpu.VMEM((HEADS_PER_STEP, TK, TQ), BF16),
                        pltpu.VMEM((HEADS_PER_STEP, HEAD_DIM + ONES_ROWS, TQ), F32),
                        pltpu.VMEM((HEADS_PER_STEP * TQ, STEP_LANES), BF16),
                        pltpu.VMEM((HEADS_PER_STEP * TQ, TK), F32),
                        pltpu.VMEM((PAIRS_PER_STEP, TQ, 2 * TK), BF16),
                        pltpu.VMEM((PAIRS_PER_STEP, TQ, PAIR_LANES), F32)],
        compiler_params=pltpu.CompilerParams(dimension_semantics=("arbitrary",) * 3),
        name="attention",
    )(qkv_fox, qkv_fox, qkv_fox, ctok, ct, qkv_sb, qkv_sb, qkv_sb, *consts)


def _post_kernel(x_ref, of_ref, os_ref, gl_ref, p_ref, bg_ref, wbf_ref, wbs_ref, wout_ref,
                 gmlp_ref, wup_ref, wdown_ref, gple_ref, wpg_ref, wple_ref, gfin_ref, out_ref):
    o_fox = jnp.dot(of_ref[...], wbf_ref[...], preferred_element_type=F32)
    o_sb = jnp.dot(os_ref[...], wbs_ref[...], preferred_element_type=F32)
    gate_a = jax.nn.sigmoid(gl_ref[:, :D_MODEL].astype(F32) + bg_ref[0:1, :])
    gate_b = jax.nn.sigmoid(gl_ref[:, D_MODEL:].astype(F32) + bg_ref[1:2, :])
    merged = (gate_a * o_fox + gate_b * o_sb).astype(BF16)
    x1 = x_ref[...] + jnp.dot(merged, wout_ref[...], preferred_element_type=F32)

    h = _rms_scale(x1, gmlp_ref[...]).astype(BF16)
    mlp = jnp.zeros_like(x1)
    for c in range(D_FF // FF_CHUNK):
        cols = slice(c * FF_CHUNK, (c + 1) * FF_CHUNK)
        up = jnp.maximum(jnp.dot(h, wup_ref[:, cols], preferred_element_type=F32), 0.0)
        mlp = mlp + jnp.dot((up * up).astype(BF16), wdown_ref[cols, :], preferred_element_type=F32)
    x2 = x1 + mlp

    h = _rms_scale(x2, gple_ref[...]).astype(BF16)
    gate = jax.nn.sigmoid(jnp.dot(h, wpg_ref[...], preferred_element_type=F32))
    emb = jnp.dot(p_ref[...].astype(BF16), wple_ref[...], preferred_element_type=F32)
    x3 = x2 + gate * emb
    out_ref[...] = _rms_scale(x3, gfin_ref[...])


def _post(x2, o_fox, o_sb, gl, p2, consts):
    t = x2.shape[0]
    tm = TM_POST
    row = lambda n: pl.BlockSpec((tm, n), lambda i: (i, 0))
    w_bytes = sum(int(c.size) * c.dtype.itemsize for c in consts)
    tile_bytes = tm * (4 * D_MODEL + 2 * D_ATT + 2 * D_ATT + 2 * 2 * D_MODEL + 4 * D_PLE + 4 * D_MODEL)
    scratch_bytes = tm * (6 * 4 * D_MODEL + 6 * FF_CHUNK)
    return pl.pallas_call(
        _post_kernel,
        grid=(t // tm,),
        in_specs=[row(D_MODEL), row(D_ATT), row(D_ATT), row(2 * D_MODEL), row(D_PLE)]
                 + [_const_spec(c.shape) for c in consts],
        out_specs=row(D_MODEL),
        out_shape=jax.ShapeDtypeStruct((t, D_MODEL), F32),
        compiler_params=pltpu.CompilerParams(
            dimension_semantics=("arbitrary",),
            vmem_limit_bytes=w_bytes + 2 * tile_bytes + scratch_bytes),
        name="post",
    )(x2, o_fox, o_sb, gl, p2, *consts)


def kernel(x, p, g_mix, w_in, b_forget, b_gate, w_branch_fox, w_branch_sb, w_out,
           g_mlp, w_up, w_down, g_ple, w_ple_gate, w_ple, g_final):
    b, s_len, d = x.shape
    t = b * s_len
    depth = w_in.shape[0]
    assert depth == 1, "the post kernel fuses the final RMSNorm into the (single) layer"
    x2 = x.reshape(t, d)
    for i in range(depth):
        o0, o1, o2 = 3 * D_ATT, 3 * D_ATT + N_HEADS, 6 * D_ATT + N_HEADS
        wa, wf, wb, wg = (w_in[i][:, lo:hi].astype(BF16) for lo, hi in ((0, o0), (o0, o1), (o1, o2), (o2, None)))
        wf = jnp.pad(wf, ((0, 0), (0, LANES - N_HEADS)))
        b_pad = jnp.pad(b_forget[i], (0, LANES - N_HEADS)).reshape(1, LANES)

        qkv_fox, f_log, qkv_sb, gl = _in_proj(x2, g_mix[i].reshape(1, d), wa, wf, wb, wg)
        ctok, ct = _forget_cumsum(f_log.reshape(b, s_len, LANES), b_pad)
        o_fox, o_sb = _attention(qkv_fox.reshape(b, s_len, 3 * D_ATT), qkv_sb.reshape(b, s_len, 3 * D_ATT),
                                 ctok, ct)

        consts = (b_gate[i], w_branch_fox[i].astype(BF16), w_branch_sb[i].astype(BF16),
                  w_out[i].astype(BF16), g_mlp[i].reshape(1, d), w_up[i].astype(BF16),
                  w_down[i].astype(BF16), g_ple[i].reshape(1, d), w_ple_gate[i].astype(BF16),
                  w_ple[i].astype(BF16), g_final.reshape(1, d))
        x2 = _post(x2, o_fox.reshape(t, D_ATT), o_sb.reshape(t, D_ATT), gl,
                   p[i].reshape(t, D_PLE), consts)
    return x2.reshape(b, s_len, d)
```

```python
import functools

import jax
import jax.numpy as jnp
import numpy as np
from jax import lax
from jax.experimental import pallas as pl
from jax.experimental.pallas import tpu as pltpu

D_MODEL = 1024
HEAD_DIM = 64
N_HEADS = 8
D_ATT = N_HEADS * HEAD_DIM
D_FF = 4 * D_MODEL
D_PLE = 256
EPS = 1e-6
SCALE = HEAD_DIM ** -0.5
LOG2E = 1.4426950408889634

LANES = 128
SUBLANES = 8
HEADS_PER_STEP = 4
PAIRS_PER_STEP = HEADS_PER_STEP // 2
PAIR_LANES = 2 * HEAD_DIM
STEP_LANES = HEADS_PER_STEP * HEAD_DIM
NEG_BIG = -1e30
ONES_ROWS = 16
SB_DEAD_CARRY = 90.0

TM_PROJ = 512
TM_POST = 512
FF_CHUNK = 512
TQ = 256
TK = 256
CUM_BLK = 128

BF16 = jnp.bfloat16
F32 = jnp.float32

_NT = (((1,), (1,)), ((), ()))


def _const_spec(shape):
    return pl.BlockSpec(shape, lambda *_: (0,) * len(shape), pipeline_mode=pl.Buffered(1))


def _rms_scale(x, g):
    r = lax.rsqrt(jnp.mean(x * x, axis=-1, keepdims=True) + EPS)
    return x * r * g


def _in_proj_kernel(x_ref, g_ref, wa_ref, wf_ref, wb_ref, wg_ref,
                    qkva_ref, f_ref, qkvb_ref, gl_ref):
    h = _rms_scale(x_ref[...], g_ref[...]).astype(BF16)
    for w_ref, out_ref, q_scale in ((wa_ref, qkva_ref, SCALE * LOG2E), (wb_ref, qkvb_ref, SCALE)):
        qkv = jnp.dot(h, w_ref[...], preferred_element_type=F32)
        out_ref[:, :D_ATT] = (qkv[:, :D_ATT] * q_scale).astype(BF16)
        out_ref[:, D_ATT:] = qkv[:, D_ATT:].astype(BF16)
    f_ref[...] = jnp.dot(h, wf_ref[...], preferred_element_type=F32)
    gl_ref[...] = jnp.dot(h, wg_ref[...], preferred_element_type=F32).astype(BF16)


def _in_proj(x2, g_mix, wa, wf, wb, wg):
    t = x2.shape[0]
    tm = TM_PROJ
    row = lambda n: pl.BlockSpec((tm, n), lambda i: (i, 0))
    w_bytes = 2 * D_MODEL * (wa.shape[1] + wf.shape[1] + wb.shape[1] + wg.shape[1])
    tile_bytes = tm * (4 * D_MODEL + 2 * wa.shape[1] + 4 * wf.shape[1] + 2 * wb.shape[1] + 2 * wg.shape[1])
    scratch_bytes = 4 * tm * wg.shape[1] + 2 * tm * D_MODEL
    return pl.pallas_call(
        _in_proj_kernel,
        grid=(t // tm,),
        in_specs=[row(D_MODEL), _const_spec(g_mix.shape), _const_spec(wa.shape), _const_spec(wf.shape),
                  _const_spec(wb.shape), _const_spec(wg.shape)],
        out_specs=[row(wa.shape[1]), row(wf.shape[1]), row(wb.shape[1]), row(wg.shape[1])],
        out_shape=[jax.ShapeDtypeStruct((t, wa.shape[1]), BF16),
                   jax.ShapeDtypeStruct((t, wf.shape[1]), F32),
                   jax.ShapeDtypeStruct((t, wb.shape[1]), BF16),
                   jax.ShapeDtypeStruct((t, wg.shape[1]), BF16)],
        compiler_params=pltpu.CompilerParams(
            dimension_semantics=("arbitrary",),
            vmem_limit_bytes=w_bytes + 2 * tile_bytes + 2 * scratch_bytes),
        name="in_proj",
    )(x2, g_mix, wa, wf, wb, wg)


def _split3(a):
    a1 = a.astype(BF16)
    r1 = a - a1.astype(F32)
    a2 = r1.astype(BF16)
    a3 = (r1 - a2.astype(F32)).astype(BF16)
    return a1, a2, a3


def _forget_cumsum_kernel(f_ref, b_ref, ctok_ref, ct_ref):
    s_len = f_ref.shape[1]
    row = lax.broadcasted_iota(jnp.int32, (CUM_BLK, CUM_BLK), 0)
    col = lax.broadcasted_iota(jnp.int32, (CUM_BLK, CUM_BLK), 1)
    tri = (col <= row).astype(BF16)
    carry = jnp.zeros((1, LANES), F32)
    for blk in range(s_len // CUM_BLK):
        rows = pl.ds(blk * CUM_BLK, CUM_BLK)
        y = f_ref[0, rows, :] + b_ref[...]
        log_f = -(jnp.maximum(-y, 0.0) + jnp.log(1.0 + jnp.exp(-jnp.abs(y))))
        c = carry
        for part in _split3(log_f):
            c = c + jnp.dot(tri, part, preferred_element_type=F32)
        carry = c[CUM_BLK - 1:CUM_BLK, :]
        c2 = c * LOG2E
        ctok_ref[0, rows, :] = c2
        ct_ref[0, :, rows] = jnp.transpose(c2)[:SUBLANES, :]


def _forget_cumsum(f3, b_pad):
    b, s_len, _ = f3.shape
    return pl.pallas_call(
        _forget_cumsum_kernel,
        grid=(b,),
        in_specs=[pl.BlockSpec((1, s_len, LANES), lambda i: (i, 0, 0)), _const_spec(b_pad.shape)],
        out_specs=[pl.BlockSpec((1, s_len, LANES), lambda i: (i, 0, 0)),
                   pl.BlockSpec((1, SUBLANES, s_len), lambda i: (i, 0, 0))],
        out_shape=[jax.ShapeDtypeStruct((b, s_len, LANES), F32),
                   jax.ShapeDtypeStruct((b, SUBLANES, s_len), F32)],
        compiler_params=pltpu.CompilerParams(dimension_semantics=("arbitrary",)),
        name="forget_cumsum",
    )(f3, b_pad)


def _head_mask():
    lane = np.arange(STEP_LANES)[None, :] // HEAD_DIM
    head = np.repeat(np.arange(HEADS_PER_STEP), TQ)[:, None]
    return jnp.asarray((lane == head).astype(np.float32), BF16)


def _pair_mask():
    lane = np.arange(PAIR_LANES)[None, :] // HEAD_DIM
    half = np.repeat(np.arange(2), TK)[:, None]
    return jnp.asarray((lane == half).astype(np.float32), BF16)


def _causal_bias(strict, keys_on_rows):
    key_pos, q_pos = np.arange(TK)[:, None], np.arange(TQ)[None, :]
    allowed = (key_pos < q_pos) if strict else (key_pos <= q_pos)
    bias = np.where(allowed, 0.0, NEG_BIG).astype(np.float32)
    return jnp.asarray(bias if keys_on_rows else bias.T, F32)


def _stack_masked_queries(q_ref, head_mask_ref, qs_ref):
    qs_ref[...] = jnp.concatenate([q_ref[0]] * HEADS_PER_STEP, axis=0) * head_mask_ref[...]


def _key_rows(j):
    return pl.ds(pl.multiple_of(j * TK, TK), TK)


_HALVES = tuple(slice(i * PAIRS_PER_STEP * TQ, (i + 1) * PAIRS_PER_STEP * TQ) for i in range(2))


def _fox_stages(group, qi, k_ref, ct_ref, causal_t_ref, qs_ref, vt_ref, ckb_ref, s_ref, p_ref, acc_ref):
    sub = lax.broadcasted_iota(jnp.int32, (SUBLANES, TQ), 0)
    ones_rows = jnp.ones((ONES_ROWS, TK), BF16)
    c8 = ct_ref[0, :, pl.ds(pl.multiple_of(qi * TQ, TQ), TQ)]
    c_q = [jnp.sum(jnp.where(sub == HEADS_PER_STEP * group + h, c8, 0.0), axis=0, keepdims=True)
           for h in range(HEADS_PER_STEP)]

    def scores(j):
        k4 = k_ref[0, _key_rows(j), :]
        for h in range(HEADS_PER_STEP):
            s_ref[h] = lax.dot_general(k4, qs_ref[h * TQ:(h + 1) * TQ, :], _NT,
                                       preferred_element_type=F32)

    def values(j, alphas):
        for h in range(HEADS_PER_STEP):
            lhs = jnp.concatenate([vt_ref[j, h * HEAD_DIM:(h + 1) * HEAD_DIM, :], ones_rows], axis=0)
            acc_ref[h] = alphas[h] * acc_ref[h] + jnp.dot(lhs, p_ref[h], preferred_element_type=F32)

    def softmax_block(j, m_run, masked):
        m_next, alphas = [], []
        for h in range(HEADS_PER_STEP):
            c_k = ckb_ref[h, _key_rows(j), :]
            u = s_ref[h] - jnp.concatenate([c_k, c_k], axis=1)
            if masked:
                u = u + causal_t_ref[...]
            m_new = jnp.maximum(m_run[h], jnp.max(u, axis=0, keepdims=True) + c_q[h])
            alphas.append(jnp.exp2(m_run[h] - m_new))
            p_ref[h] = jnp.exp2(u - (m_new - c_q[h])).astype(BF16)
            m_next.append(m_new)
        return tuple(m_next), tuple(alphas)

    return scores, values, softmax_block


def _sb_stages(k_ref, v_ref, pair_mask_ref, causal_ref, tri_ref, qs_ref, z_ref, w_ref, acc_ref):
    pair_mask = pair_mask_ref[...]

    def scores(j):
        k4 = k_ref[0, _key_rows(j), :]
        for half in _HALVES:
            z_ref[half, :] = lax.dot_general(qs_ref[half, :], k4, _NT, preferred_element_type=F32)

    def values(j):
        v4 = v_ref[0, _key_rows(j), :]
        for p in range(PAIRS_PER_STEP):
            v_pair = v4[:, p * PAIR_LANES:(p + 1) * PAIR_LANES]
            vbd = jnp.concatenate([v_pair, v_pair], axis=0) * pair_mask
            acc_ref[p] += jnp.dot(w_ref[p], vbd, preferred_element_type=F32)

    def weights(carry, masked):
        log_betas, sps = [], []
        for h in range(HEADS_PER_STEP):
            z = z_ref[h * TQ:(h + 1) * TQ, :]
            if masked:
                z = z + causal_ref[...]
            log_beta = jnp.minimum(z, 0.0) - jnp.log(1.0 + jnp.exp2(jnp.abs(z) * -LOG2E))
            log_betas.append(log_beta)
            sps.append((z - log_beta).astype(BF16))
        csums = [jnp.dot(jnp.concatenate(sps[2 * p:2 * p + 2], axis=0), tri_ref[...],
                         preferred_element_type=F32) for p in range(PAIRS_PER_STEP)]
        carry_next = []
        for h in range(HEADS_PER_STEP):
            csum = csums[h // 2][(h % 2) * TQ:(h % 2 + 1) * TQ]
            w_ref[h // 2, :, (h % 2) * TK:(h % 2 + 1) * TK] = (
                jnp.exp(log_betas[h] - csum - carry[h]).astype(BF16))
            carry_next.append(carry[h] + (csum[:, 0:1] + sps[h][:, 0:1].astype(F32)))
        return tuple(carry_next)

    return scores, values, weights


def _attn_kernel(fq_ref, fk_ref, fv_ref, ctok_ref, ct_ref, sq_ref, sk_ref, sv_ref,
                 head_mask_ref, fox_causal_ref, pair_mask_ref, sb_causal_ref, tri_ref,
                 fo_ref, so_ref,
                 fqs_ref, vt_ref, ckb_ref, s_ref, p_ref, facc_ref, sqs_ref, z_ref, w_ref, sacc_ref):
    group = pl.program_id(1)
    qi = pl.program_id(2)

    @pl.when(qi == 0)
    def _():
        lane = lax.broadcasted_iota(jnp.int32, (TK, LANES), 1)
        for j in range(fk_ref.shape[1] // TK):
            rows = slice(j * TK, (j + 1) * TK)
            vt_ref[j] = jnp.transpose(fv_ref[0, rows, :].astype(F32)).astype(BF16)
            ctok = ctok_ref[0, rows, :]
            for h in range(HEADS_PER_STEP):
                c_col = jnp.sum(jnp.where(lane == HEADS_PER_STEP * group + h, ctok, 0.0),
                                axis=-1, keepdims=True)
                ckb_ref[h, rows, :] = jnp.broadcast_to(c_col, (TK, LANES))

    _stack_masked_queries(fq_ref, head_mask_ref, fqs_ref)
    _stack_masked_queries(sq_ref, head_mask_ref, sqs_ref)
    facc_ref[...] = jnp.zeros(facc_ref.shape, F32)
    sacc_ref[...] = jnp.zeros(sacc_ref.shape, F32)
    fox_scores, fox_values, fox_softmax = _fox_stages(
        group, qi, fk_ref, ct_ref, fox_causal_ref, fqs_ref, vt_ref, ckb_ref, s_ref, p_ref, facc_ref)
    sb_scores, sb_values, sb_weights = _sb_stages(
        sk_ref, sv_ref, pair_mask_ref, sb_causal_ref, tri_ref, sqs_ref, z_ref, w_ref, sacc_ref)

    def still_live(carry):
        lowest = functools.reduce(jnp.minimum, carry)
        return jnp.min(lowest) < SB_DEAD_CARRY

    fox_scores(qi)
    sb_scores(qi)
    m_init = tuple(jnp.full((1, TQ), NEG_BIG, F32) for _ in range(HEADS_PER_STEP))
    fox_state = fox_softmax(qi, m_init, True)
    carry = sb_weights(tuple(jnp.zeros((TQ, 1), F32) for _ in range(HEADS_PER_STEP)), True)
    nxt = jnp.maximum(qi - 1, 0)
    fox_scores(nxt)
    sb_scores(nxt)

    def fox_step(t, state):
        m_run, alphas = state
        fox_values(qi - t + 1, alphas)
        state = fox_softmax(qi - t, m_run, False)
        fox_scores(jnp.maximum(qi - t - 1, 0))
        return state

    def both_step(state):
        t, (m_run, alphas), carry, _ = state
        fox_values(qi - t + 1, alphas)
        sb_values(qi - t + 1)
        fox_state = fox_softmax(qi - t, m_run, False)
        carry = sb_weights(carry, False)
        nxt = jnp.maximum(qi - t - 1, 0)
        fox_scores(nxt)
        sb_scores(nxt)
        return t + 1, fox_state, carry, still_live(carry)

    t_sb, fox_state, _, _ = lax.while_loop(lambda st: (st[0] <= qi) & st[3], both_step,
                                           (jnp.int32(1), fox_state, carry, still_live(carry)))
    _, alphas = lax.fori_loop(t_sb, qi + 1, fox_step, fox_state)

    fox_values(0, alphas)
    sb_values(qi - t_sb + 1)
    out_t = jnp.concatenate([facc_ref[h, :HEAD_DIM, :] / facc_ref[h, HEAD_DIM:HEAD_DIM + 1, :]
                             for h in range(HEADS_PER_STEP)], axis=0)
    fo_ref[0] = jnp.transpose(out_t).astype(BF16)
    so_ref[0] = jnp.concatenate([sacc_ref[p] for p in range(PAIRS_PER_STEP)], axis=1).astype(BF16)


def _attention(qkv_fox, qkv_sb, ctok, ct):
    b, s_len, _ = qkv_fox.shape
    groups = D_ATT // STEP_LANES
    q_spec = pl.BlockSpec((1, TQ, STEP_LANES), lambda b, g, i: (b, i, g))
    k_spec = pl.BlockSpec((1, s_len, STEP_LANES), lambda b, g, i: (b, 0, groups + g))
    v_spec = pl.BlockSpec((1, s_len, STEP_LANES), lambda b, g, i: (b, 0, 2 * groups + g))
    o_spec = pl.BlockSpec((1, TQ, STEP_LANES), lambda b, g, i: (b, i, g))
    tri = jnp.asarray(np.tril(np.ones((TK, TK), np.float32), -1), BF16)
    consts = (_head_mask(), _causal_bias(strict=False, keys_on_rows=True), _pair_mask(),
              _causal_bias(strict=True, keys_on_rows=False), tri)
    out = jax.ShapeDtypeStruct((b, s_len, D_ATT), BF16)
    return pl.pallas_call(
        _attn_kernel,
        grid=(b, groups, s_len // TQ),
        in_specs=[q_spec, k_spec, v_spec,
                  pl.BlockSpec((1, s_len, LANES), lambda b, g, i: (b, 0, 0)),
                  pl.BlockSpec((1, SUBLANES, s_len), lambda b, g, i: (b, 0, 0)),
                  q_spec, k_spec, v_spec] + [_const_spec(c.shape) for c in consts],
        out_specs=[o_spec, o_spec],
        out_shape=[out, out],
        scratch_shapes=[pltpu.VMEM((HEADS_PER_STEP * TQ, STEP_LANES), BF16),
                        pltpu.VMEM((s_len // TK, STEP_LANES, TK), BF16),
                        pltpu.VMEM((HEADS_PER_STEP, s_len, LANES), F32),
                        pltpu.VMEM((HEADS_PER_STEP, TK, TQ), F32),
                        pltpu.VMEM((HEADS_PER_STEP, TK, TQ), BF16),
                        pltpu.VMEM((HEADS_PER_STEP, HEAD_DIM + ONES_ROWS, TQ), F32),
                        pltpu.VMEM((HEADS_PER_STEP * TQ, STEP_LANES), BF16),
                        pltpu.VMEM((HEADS_PER_STEP * TQ, TK), F32),
                        pltpu.VMEM((PAIRS_PER_STEP, TQ, 2 * TK), BF16),
                        pltpu.VMEM((PAIRS_PER_STEP, TQ, PAIR_LANES), F32)],
        compiler_params=pltpu.CompilerParams(dimension_semantics=("arbitrary",) * 3),
        name="attention",
    )(qkv_fox, qkv_fox, qkv_fox, ctok, ct, qkv_sb, qkv_sb, qkv_sb, *consts)


def _post_kernel(x_ref, of_ref, os_ref, gl_ref, p_ref, bg_ref, wbf_ref, wbs_ref, wout_ref,
                 gmlp_ref, wup_ref, wdown_ref, gple_ref, wpg_ref, wple_ref, gfin_ref, out_ref):
    o_fox = jnp.dot(of_ref[...], wbf_ref[...], preferred_element_type=F32)
    o_sb = jnp.dot(os_ref[...], wbs_ref[...], preferred_element_type=F32)
    gate_a = jax.nn.sigmoid(gl_ref[:, :D_MODEL].astype(F32) + bg_ref[0:1, :])
    gate_b = jax.nn.sigmoid(gl_ref[:, D_MODEL:].astype(F32) + bg_ref[1:2, :])
    merged = (gate_a * o_fox + gate_b * o_sb).astype(BF16)
    x1 = x_ref[...] + jnp.dot(merged, wout_ref[...], preferred_element_type=F32)

    h = _rms_scale(x1, gmlp_ref[...]).astype(BF16)
    mlp = jnp.zeros_like(x1)
    for c in range(D_FF // FF_CHUNK):
        cols = slice(c * FF_CHUNK, (c + 1) * FF_CHUNK)
        up = jnp.maximum(jnp.dot(h, wup_ref[:, cols], preferred_element_type=F32), 0.0)
        mlp = mlp + jnp.dot((up * up).astype(BF16), wdown_ref[cols, :], preferred_element_type=F32)
    x2 = x1 + mlp

    h = _rms_scale(x2, gple_ref[...]).astype(BF16)
    gate = jax.nn.sigmoid(jnp.dot(h, wpg_ref[...], preferred_element_type=F32))
    emb = jnp.dot(p_ref[...].astype(BF16), wple_ref[...], preferred_element_type=F32)
    x3 = x2 + gate * emb
    out_ref[...] = _rms_scale(x3, gfin_ref[...])


def _post(x2, o_fox, o_sb, gl, p2, consts):
    t = x2.shape[0]
    tm = TM_POST
    row = lambda n: pl.BlockSpec((tm, n), lambda i: (i, 0))
    w_bytes = sum(int(c.size) * c.dtype.itemsize for c in consts)
    tile_bytes = tm * (4 * D_MODEL + 2 * D_ATT + 2 * D_ATT + 2 * 2 * D_MODEL + 4 * D_PLE + 4 * D_MODEL)
    scratch_bytes = tm * (6 * 4 * D_MODEL + 6 * FF_CHUNK)
    return pl.pallas_call(
        _post_kernel,
        grid=(t // tm,),
        in_specs=[row(D_MODEL), row(D_ATT), row(D_ATT), row(2 * D_MODEL), row(D_PLE)]
                 + [_const_spec(c.shape) for c in consts],
        out_specs=row(D_MODEL),
        out_shape=jax.ShapeDtypeStruct((t, D_MODEL), F32),
        compiler_params=pltpu.CompilerParams(
            dimension_semantics=("arbitrary",),
            vmem_limit_bytes=w_bytes + 2 * tile_bytes + scratch_bytes),
        name="post",
    )(x2, o_fox, o_sb, gl, p2, *consts)


def kernel(x, p, g_mix, w_in, b_forget, b_gate, w_branch_fox, w_branch_sb, w_out,
           g_mlp, w_up, w_down, g_ple, w_ple_gate, w_ple, g_final):
    b, s_len, d = x.shape
    t = b * s_len
    depth = w_in.shape[0]
    assert depth == 1, "the post kernel fuses the final RMSNorm into the (single) layer"
    x2 = x.reshape(t, d)
    for i in range(depth):
        o0, o1, o2 = 3 * D_ATT, 3 * D_ATT + N_HEADS, 6 * D_ATT + N_HEADS
        wa, wf, wb, wg = (w_in[i][:, lo:hi].astype(BF16) for lo, hi in ((0, o0), (o0, o1), (o1, o2), (o2, None)))
        wf = jnp.pad(wf, ((0, 0), (0, LANES - N_HEADS)))
        b_pad = jnp.pad(b_forget[i], (0, LANES - N_HEADS)).reshape(1, LANES)

        qkv_fox, f_log, qkv_sb, gl = _in_proj(x2, g_mix[i].reshape(1, d), wa, wf, wb, wg)
        ctok, ct = _forget_cumsum(f_log.reshape(b, s_len, LANES), b_pad)
        o_fox, o_sb = _attention(qkv_fox.reshape(b, s_len, 3 * D_ATT), qkv_sb.reshape(b, s_len, 3 * D_ATT),
                                 ctok, ct)

        consts = (b_gate[i], w_branch_fox[i].astype(BF16), w_branch_sb[i].astype(BF16),
                  w_out[i].astype(BF16), g_mlp[i].reshape(1, d), w_up[i].astype(BF16),
                  w_down[i].astype(BF16), g_ple[i].reshape(1, d), w_ple_gate[i].astype(BF16),
                  w_ple[i].astype(BF16), g_final.reshape(1, d))
        x2 = _post(x2, o_fox.reshape(t, D_ATT), o_sb.reshape(t, D_ATT), gl,
                   p[i].reshape(t, D_PLE), consts)
    return x2.reshape(b, s_len, d)
```

```python
import functools

import jax
import jax.numpy as jnp
import numpy as np
from jax import lax
from jax.experimental import pallas as pl
from jax.experimental.pallas import tpu as pltpu

D_MODEL = 1024
HEAD_DIM = 64
N_HEADS = 8
D_ATT = N_HEADS * HEAD_DIM
D_FF = 4 * D_MODEL
D_PLE = 256
EPS = 1e-6
SCALE = HEAD_DIM ** -0.5
LOG2E = 1.4426950408889634

LANES = 128
SUBLANES = 8
HEADS_PER_STEP = 4
PAIRS_PER_STEP = HEADS_PER_STEP // 2
PAIR_LANES = 2 * HEAD_DIM
STEP_LANES = HEADS_PER_STEP * HEAD_DIM
NEG_BIG = -1e30
ONES_ROWS = 16
SB_DEAD_CARRY = 90.0

TM_PROJ = 512
TM_POST = 512
FF_CHUNK = 512
TQ = 256
TK = 256
HALF = TK // 2
CUM_BLK = 128

BF16 = jnp.bfloat16
F32 = jnp.float32

_NT = (((1,), (1,)), ((), ()))


def _const_spec(shape):
    return pl.BlockSpec(shape, lambda *_: (0,) * len(shape), pipeline_mode=pl.Buffered(1))


def _rms_scale(x, g):
    r = lax.rsqrt(jnp.mean(x * x, axis=-1, keepdims=True) + EPS)
    return x * r * g


def _in_proj_kernel(x_ref, g_ref, wa_ref, wf_ref, wb_ref, wg_ref,
                    qkva_ref, f_ref, qkvb_ref, gl_ref):
    h = _rms_scale(x_ref[...], g_ref[...]).astype(BF16)
    for w_ref, out_ref, q_scale in ((wa_ref, qkva_ref, SCALE * LOG2E), (wb_ref, qkvb_ref, SCALE)):
        qkv = jnp.dot(h, w_ref[...], preferred_element_type=F32)
        out_ref[:, :D_ATT] = (qkv[:, :D_ATT] * q_scale).astype(BF16)
        out_ref[:, D_ATT:] = qkv[:, D_ATT:].astype(BF16)
    f_ref[...] = jnp.dot(h, wf_ref[...], preferred_element_type=F32)
    gl_ref[...] = jnp.dot(h, wg_ref[...], preferred_element_type=F32).astype(BF16)


def _in_proj(x2, g_mix, wa, wf, wb, wg):
    t = x2.shape[0]
    tm = TM_PROJ
    row = lambda n: pl.BlockSpec((tm, n), lambda i: (i, 0))
    w_bytes = 2 * D_MODEL * (wa.shape[1] + wf.shape[1] + wb.shape[1] + wg.shape[1])
    tile_bytes = tm * (4 * D_MODEL + 2 * wa.shape[1] + 4 * wf.shape[1] + 2 * wb.shape[1] + 2 * wg.shape[1])
    scratch_bytes = 4 * tm * wg.shape[1] + 2 * tm * D_MODEL
    return pl.pallas_call(
        _in_proj_kernel,
        grid=(t // tm,),
        in_specs=[row(D_MODEL), _const_spec(g_mix.shape), _const_spec(wa.shape), _const_spec(wf.shape),
                  _const_spec(wb.shape), _const_spec(wg.shape)],
        out_specs=[row(wa.shape[1]), row(wf.shape[1]), row(wb.shape[1]), row(wg.shape[1])],
        out_shape=[jax.ShapeDtypeStruct((t, wa.shape[1]), BF16),
                   jax.ShapeDtypeStruct((t, wf.shape[1]), F32),
                   jax.ShapeDtypeStruct((t, wb.shape[1]), BF16),
                   jax.ShapeDtypeStruct((t, wg.shape[1]), BF16)],
        compiler_params=pltpu.CompilerParams(
            dimension_semantics=("arbitrary",),
            vmem_limit_bytes=w_bytes + 2 * tile_bytes + 2 * scratch_bytes),
        name="in_proj",
    )(x2, g_mix, wa, wf, wb, wg)


def _split3(a):
    a1 = a.astype(BF16)
    r1 = a - a1.astype(F32)
    a2 = r1.astype(BF16)
    a3 = (r1 - a2.astype(F32)).astype(BF16)
    return a1, a2, a3


def _forget_cumsum_kernel(f_ref, b_ref, ctok_ref, ct_ref):
    s_len = f_ref.shape[1]
    row = lax.broadcasted_iota(jnp.int32, (CUM_BLK, CUM_BLK), 0)
    col = lax.broadcasted_iota(jnp.int32, (CUM_BLK, CUM_BLK), 1)
    tri = (col <= row).astype(BF16)
    carry = jnp.zeros((1, LANES), F32)
    for blk in range(s_len // CUM_BLK):
        rows = pl.ds(blk * CUM_BLK, CUM_BLK)
        y = f_ref[0, rows, :] + b_ref[...]
        log_f = -(jnp.maximum(-y, 0.0) + jnp.log(1.0 + jnp.exp(-jnp.abs(y))))
        c = carry
        for part in _split3(log_f):
            c = c + jnp.dot(tri, part, preferred_element_type=F32)
        carry = c[CUM_BLK - 1:CUM_BLK, :]
        c2 = c * LOG2E
        ctok_ref[0, rows, :] = c2
        ct_ref[0, :, rows] = jnp.transpose(c2)[:SUBLANES, :]


def _forget_cumsum(f3, b_pad):
    b, s_len, _ = f3.shape
    return pl.pallas_call(
        _forget_cumsum_kernel,
        grid=(b,),
        in_specs=[pl.BlockSpec((1, s_len, LANES), lambda i: (i, 0, 0)), _const_spec(b_pad.shape)],
        out_specs=[pl.BlockSpec((1, s_len, LANES), lambda i: (i, 0, 0)),
                   pl.BlockSpec((1, SUBLANES, s_len), lambda i: (i, 0, 0))],
        out_shape=[jax.ShapeDtypeStruct((b, s_len, LANES), F32),
                   jax.ShapeDtypeStruct((b, SUBLANES, s_len), F32)],
        compiler_params=pltpu.CompilerParams(dimension_semantics=("arbitrary",)),
        name="forget_cumsum",
    )(f3, b_pad)


def _head_mask():
    lane = np.arange(STEP_LANES)[None, :] // HEAD_DIM
    head = np.repeat(np.arange(HEADS_PER_STEP), TQ)[:, None]
    return jnp.asarray((lane == head).astype(np.float32), BF16)


def _pair_mask():
    lane = np.arange(PAIR_LANES)[None, :] // HEAD_DIM
    half = np.repeat(np.arange(2), TK)[:, None]
    return jnp.asarray((lane == half).astype(np.float32), BF16)


def _causal_bias(strict, keys_on_rows):
    key_pos, q_pos = np.arange(TK)[:, None], np.arange(TQ)[None, :]
    allowed = (key_pos < q_pos) if strict else (key_pos <= q_pos)
    bias = np.where(allowed, 0.0, NEG_BIG).astype(np.float32)
    return jnp.asarray(bias if keys_on_rows else bias.T, F32)


def _stack_masked_queries(q_ref, head_mask_ref, qs_ref):
    qs_ref[...] = jnp.concatenate([q_ref[0]] * HEADS_PER_STEP, axis=0) * head_mask_ref[...]


def _key_rows(j):
    return pl.ds(pl.multiple_of(j * TK, TK), TK)


_HALVES = tuple(slice(i * PAIRS_PER_STEP * TQ, (i + 1) * PAIRS_PER_STEP * TQ) for i in range(2))


def _fox_stages(group, qi, k_ref, ct_ref, causal_t_ref, qs_ref, vt_ref, ckb_ref, s_ref, p_ref, acc_ref):
    sub = lax.broadcasted_iota(jnp.int32, (SUBLANES, TQ), 0)
    ones_rows = jnp.ones((ONES_ROWS, TK), BF16)
    c8 = ct_ref[0, :, pl.ds(pl.multiple_of(qi * TQ, TQ), TQ)]
    c_q = [jnp.sum(jnp.where(sub == HEADS_PER_STEP * group + h, c8, 0.0), axis=0, keepdims=True)
           for h in range(HEADS_PER_STEP)]

    def scores(j):
        k4 = k_ref[0, _key_rows(j), :]
        for h in range(HEADS_PER_STEP):
            s_ref[h] = lax.dot_general(k4, qs_ref[h * TQ:(h + 1) * TQ, :], _NT,
                                       preferred_element_type=F32)

    def values(j, alphas):
        for h in range(HEADS_PER_STEP):
            lhs = jnp.concatenate([vt_ref[j, h * HEAD_DIM:(h + 1) * HEAD_DIM, :], ones_rows], axis=0)
            acc_ref[h] = alphas[h] * acc_ref[h] + jnp.dot(lhs, p_ref[h], preferred_element_type=F32)

    def softmax_block(j, m_run):
        m_next, alphas = [], []
        for h in range(HEADS_PER_STEP):
            c_k = ckb_ref[h, _key_rows(j), :]
            u = s_ref[h] - jnp.concatenate([c_k, c_k], axis=1)
            m_new = jnp.maximum(m_run[h], jnp.max(u, axis=0, keepdims=True) + c_q[h])
            alphas.append(jnp.exp2(m_run[h] - m_new))
            p_ref[h] = jnp.exp2(u - (m_new - c_q[h])).astype(BF16)
            m_next.append(m_new)
        return tuple(m_next), tuple(alphas)

    def softmax_diagonal(m_run):
        tri = causal_t_ref[:HALF, :HALF]
        m_next, alphas = [], []
        for h in range(HEADS_PER_STEP):
            c_k = ckb_ref[h, _key_rows(qi), :]
            c_top, c_bot = c_k[:HALF], c_k[HALF:]
            u_tl = s_ref[h, :HALF, :HALF] - c_top + tri
            u_tr = s_ref[h, :HALF, HALF:] - c_top
            u_br = s_ref[h, HALF:, HALF:] - c_bot + tri
            col_max = jnp.concatenate(
                [jnp.max(u_tl, axis=0, keepdims=True),
                 jnp.maximum(jnp.max(u_tr, axis=0, keepdims=True), jnp.max(u_br, axis=0, keepdims=True))], axis=1)
            m_new = jnp.maximum(m_run[h], col_max + c_q[h])
            alphas.append(jnp.exp2(m_run[h] - m_new))
            shift = m_new - c_q[h]
            p_ref[h, :HALF, :HALF] = jnp.exp2(u_tl - shift[:, :HALF]).astype(BF16)
            p_ref[h, :HALF, HALF:] = jnp.exp2(u_tr - shift[:, HALF:]).astype(BF16)
            p_ref[h, HALF:, :HALF] = jnp.zeros((HALF, HALF), BF16)
            p_ref[h, HALF:, HALF:] = jnp.exp2(u_br - shift[:, HALF:]).astype(BF16)
            m_next.append(m_new)
        return tuple(m_next), tuple(alphas)

    return scores, values, softmax_block, softmax_diagonal


def _sb_stages(k_ref, v_ref, pair_mask_ref, causal_ref, tri_ref, qs_ref, z_ref, w_ref, acc_ref):
    pair_mask = pair_mask_ref[...]

    def scores(j):
        k4 = k_ref[0, _key_rows(j), :]
        for half in _HALVES:
            z_ref[half, :] = lax.dot_general(qs_ref[half, :], k4, _NT, preferred_element_type=F32)

    def values(j):
        v4 = v_ref[0, _key_rows(j), :]
        for p in range(PAIRS_PER_STEP):
            v_pair = v4[:, p * PAIR_LANES:(p + 1) * PAIR_LANES]
            vbd = jnp.concatenate([v_pair, v_pair], axis=0) * pair_mask
            acc_ref[p] += jnp.dot(w_ref[p], vbd, preferred_element_type=F32)

    def log_beta_and_softplus(z):
        log_beta = jnp.minimum(z, 0.0) - jnp.log(1.0 + jnp.exp2(jnp.abs(z) * -LOG2E))
        return log_beta, (z - log_beta).astype(BF16)

    def suffix_sums(sps):
        csums = [jnp.dot(jnp.concatenate(sps[2 * p:2 * p + 2], axis=0), tri_ref[...],
                         preferred_element_type=F32) for p in range(PAIRS_PER_STEP)]
        return [csums[h // 2][(h % 2) * TQ:(h % 2 + 1) * TQ] for h in range(HEADS_PER_STEP)]

    def extend(carry, csums, sps):
        return tuple(carry[h] + (csums[h][:, 0:1] + sps[h][:, 0:1].astype(F32)) for h in range(HEADS_PER_STEP))

    def weights(carry):
        log_betas, sps = zip(*(log_beta_and_softplus(z_ref[h * TQ:(h + 1) * TQ, :])
                               for h in range(HEADS_PER_STEP)))
        csums = suffix_sums(sps)
        for h in range(HEADS_PER_STEP):
            w_ref[h // 2, :, (h % 2) * TK:(h % 2 + 1) * TK] = (
                jnp.exp(log_betas[h] - csums[h] - carry[h]).astype(BF16))
        return extend(carry, csums, sps)

    def weights_diagonal(carry):
        tri = causal_ref[:HALF, :HALF]
        rows = lambda h, lo: slice(h * TQ + lo, h * TQ + lo + HALF)
        lb_tl, lb_bot, sps = [], [], []
        for h in range(HEADS_PER_STEP):
            lb, sp_tl = log_beta_and_softplus(z_ref[rows(h, 0), :HALF] + tri)
            lb_tl.append(lb)
            lb_bl, sp_bl = log_beta_and_softplus(z_ref[rows(h, HALF), :HALF])
            lb_br, sp_br = log_beta_and_softplus(z_ref[rows(h, HALF), HALF:] + tri)
            lb_bot.append(jnp.concatenate([lb_bl, lb_br], axis=1))
            sps.append(jnp.concatenate([jnp.concatenate([sp_tl, jnp.zeros_like(sp_tl)], axis=1),
                                        jnp.concatenate([sp_bl, sp_br], axis=1)], axis=0))
        csums = suffix_sums(sps)
        for h in range(HEADS_PER_STEP):
            lanes = slice((h % 2) * TK, (h % 2 + 1) * TK)
            left = slice((h % 2) * TK, (h % 2) * TK + HALF)
            right = slice((h % 2) * TK + HALF, (h % 2 + 1) * TK)
            w_ref[h // 2, :HALF, left] = jnp.exp(lb_tl[h] - csums[h][:HALF, :HALF] - carry[h][:HALF]).astype(BF16)
            w_ref[h // 2, :HALF, right] = jnp.zeros((HALF, HALF), BF16)
            w_ref[h // 2, HALF:, lanes] = jnp.exp(lb_bot[h] - csums[h][HALF:] - carry[h][HALF:]).astype(BF16)
        return extend(carry, csums, sps)

    return scores, values, weights, weights_diagonal


def _attn_kernel(fq_ref, fk_ref, fv_ref, ctok_ref, ct_ref, sq_ref, sk_ref, sv_ref,
                 head_mask_ref, fox_causal_ref, pair_mask_ref, sb_causal_ref, tri_ref,
                 fo_ref, so_ref,
                 fqs_ref, vt_ref, ckb_ref, s_ref, p_ref, facc_ref, sqs_ref, z_ref, w_ref, sacc_ref):
    group = pl.program_id(1)
    qi = pl.program_id(2)

    @pl.when(qi == 0)
    def _():
        lane = lax.broadcasted_iota(jnp.int32, (TK, LANES), 1)
        for j in range(fk_ref.shape[1] // TK):
            rows = slice(j * TK, (j + 1) * TK)
            vt_ref[j] = jnp.transpose(fv_ref[0, rows, :].astype(F32)).astype(BF16)
            ctok = ctok_ref[0, rows, :]
            for h in range(HEADS_PER_STEP):
                c_col = jnp.sum(jnp.where(lane == HEADS_PER_STEP * group + h, ctok, 0.0),
                                axis=-1, keepdims=True)
                ckb_ref[h, rows, :] = jnp.broadcast_to(c_col, (TK, LANES))

    _stack_masked_queries(fq_ref, head_mask_ref, fqs_ref)
    _stack_masked_queries(sq_ref, head_mask_ref, sqs_ref)
    facc_ref[...] = jnp.zeros(facc_ref.shape, F32)
    sacc_ref[...] = jnp.zeros(sacc_ref.shape, F32)
    fox_scores, fox_values, fox_softmax, fox_softmax_diagonal = _fox_stages(
        group, qi, fk_ref, ct_ref, fox_causal_ref, fqs_ref, vt_ref, ckb_ref, s_ref, p_ref, facc_ref)
    sb_scores, sb_values, sb_weights, sb_weights_diagonal = _sb_stages(
        sk_ref, sv_ref, pair_mask_ref, sb_causal_ref, tri_ref, sqs_ref, z_ref, w_ref, sacc_ref)

    def still_live(carry):
        lowest = functools.reduce(jnp.minimum, carry)
        return jnp.min(lowest) < SB_DEAD_CARRY

    fox_scores(qi)
    sb_scores(qi)
    m_init = tuple(jnp.full((1, TQ), NEG_BIG, F32) for _ in range(HEADS_PER_STEP))
    fox_state = fox_softmax_diagonal(m_init)
    carry = sb_weights_diagonal(tuple(jnp.zeros((TQ, 1), F32) for _ in range(HEADS_PER_STEP)))
    nxt = jnp.maximum(qi - 1, 0)
    fox_scores(nxt)
    sb_scores(nxt)

    def fox_step(t, state):
        m_run, alphas = state
        fox_values(qi - t + 1, alphas)
        state = fox_softmax(qi - t, m_run)
        fox_scores(jnp.maximum(qi - t - 1, 0))
        return state

    def both_step(state):
        t, (m_run, alphas), carry, _ = state
        fox_values(qi - t + 1, alphas)
        sb_values(qi - t + 1)
        fox_state = fox_softmax(qi - t, m_run)
        carry = sb_weights(carry)
        nxt = jnp.maximum(qi - t - 1, 0)
        fox_scores(nxt)
        sb_scores(nxt)
        return t + 1, fox_state, carry, still_live(carry)

    t_sb, fox_state, _, _ = lax.while_loop(lambda st: (st[0] <= qi) & st[3], both_step,
                                           (jnp.int32(1), fox_state, carry, jnp.bool_(True)))
    _, alphas = lax.fori_loop(t_sb, qi + 1, fox_step, fox_state)

    fox_values(0, alphas)
    sb_values(qi - t_sb + 1)
    out_t = jnp.concatenate([facc_ref[h, :HEAD_DIM, :] / facc_ref[h, HEAD_DIM:HEAD_DIM + 1, :]
                             for h in range(HEADS_PER_STEP)], axis=0)
    fo_ref[0] = jnp.transpose(out_t).astype(BF16)
    so_ref[0] = jnp.concatenate([sacc_ref[p] for p in range(PAIRS_PER_STEP)], axis=1).astype(BF16)


def _attention(qkv_fox, qkv_sb, ctok, ct):
    b, s_len, _ = qkv_fox.shape
    groups = D_ATT // STEP_LANES
    q_spec = pl.BlockSpec((1, TQ, STEP_LANES), lambda b, g, i: (b, i, g))
    k_spec = pl.BlockSpec((1, s_len, STEP_LANES), lambda b, g, i: (b, 0, groups + g))
    v_spec = pl.BlockSpec((1, s_len, STEP_LANES), lambda b, g, i: (b, 0, 2 * groups + g))
    o_spec = pl.BlockSpec((1, TQ, STEP_LANES), lambda b, g, i: (b, i, g))
    tri = jnp.asarray(np.tril(np.ones((TK, TK), np.float32), -1), BF16)
    consts = (_head_mask(), _causal_bias(strict=False, keys_on_rows=True), _pair_mask(),
              _causal_bias(strict=True, keys_on_rows=False), tri)
    out = jax.ShapeDtypeStruct((b, s_len, D_ATT), BF16)
    return pl.pallas_call(
        _attn_kernel,
        grid=(b, groups, s_len // TQ),
        in_specs=[q_spec, k_spec, v_spec,
                  pl.BlockSpec((1, s_len, LANES), lambda b, g, i: (b, 0, 0)),
                  pl.BlockSpec((1, SUBLANES, s_len), lambda b, g, i: (b, 0, 0)),
                  q_spec, k_spec, v_spec] + [_const_spec(c.shape) for c in consts],
        out_specs=[o_spec, o_spec],
        out_shape=[out, out],
        scratch_shapes=[pltpu.VMEM((HEADS_PER_STEP * TQ, STEP_LANES), BF16),
                        pltpu.VMEM((s_len // TK, STEP_LANES, TK), BF16),
                        pltpu.VMEM((HEADS_PER_STEP, s_len, LANES), F32),
                        pltpu.VMEM((HEADS_PER_STEP, TK, TQ), F32),
                        pltpu.VMEM((HEADS_PER_STEP, TK, TQ), BF16),
                        pltpu.VMEM((HEADS_PER_STEP, HEAD_DIM + ONES_ROWS, TQ), F32),
                        pltpu.VMEM((HEADS_PER_STEP * TQ, STEP_LANES), BF16),
                        pltpu.VMEM((HEADS_PER_STEP * TQ, TK), F32),
                        pltpu.VMEM((PAIRS_PER_STEP, TQ, 2 * TK), BF16),
                        pltpu.VMEM((PAIRS_PER_STEP, TQ, PAIR_LANES), F32)],
        compiler_params=pltpu.CompilerParams(dimension_semantics=("arbitrary",) * 3),
        name="attention",
    )(qkv_fox, qkv_fox, qkv_fox, ctok, ct, qkv_sb, qkv_sb, qkv_sb, *consts)


def _post_kernel(x_ref, of_ref, os_ref, gl_ref, p_ref, bg_ref, wbf_ref, wbs_ref, wout_ref,
                 gmlp_ref, wup_ref, wdown_ref, gple_ref, wpg_ref, wple_ref, gfin_ref, out_ref):
    o_fox = jnp.dot(of_ref[...], wbf_ref[...], preferred_element_type=F32)
    o_sb = jnp.dot(os_ref[...], wbs_ref[...], preferred_element_type=F32)
    gate_a = jax.nn.sigmoid(gl_ref[:, :D_MODEL].astype(F32) + bg_ref[0:1, :])
    gate_b = jax.nn.sigmoid(gl_ref[:, D_MODEL:].astype(F32) + bg_ref[1:2, :])
    merged = (gate_a * o_fox + gate_b * o_sb).astype(BF16)
    x1 = x_ref[...] + jnp.dot(merged, wout_ref[...], preferred_element_type=F32)

    h = _rms_scale(x1, gmlp_ref[...]).astype(BF16)
    mlp = jnp.zeros_like(x1)
    for c in range(D_FF // FF_CHUNK):
        cols = slice(c * FF_CHUNK, (c + 1) * FF_CHUNK)
        up = jnp.maximum(jnp.dot(h, wup_ref[:, cols], preferred_element_type=F32), 0.0)
        mlp = mlp + jnp.dot((up * up).astype(BF16), wdown_ref[cols, :], preferred_element_type=F32)
    x2 = x1 + mlp

    h = _rms_scale(x2, gple_ref[...]).astype(BF16)
    gate = jax.nn.sigmoid(jnp.dot(h, wpg_ref[...], preferred_element_type=F32))
    emb = jnp.dot(p_ref[...].astype(BF16), wple_ref[...], preferred_element_type=F32)
    x3 = x2 + gate * emb
    out_ref[...] = _rms_scale(x3, gfin_ref[...])


def _post(x2, o_fox, o_sb, gl, p2, consts):
    t = x2.shape[0]
    tm = TM_POST
    row = lambda n: pl.BlockSpec((tm, n), lambda i: (i, 0))
    w_bytes = sum(int(c.size) * c.dtype.itemsize for c in consts)
    tile_bytes = tm * (4 * D_MODEL + 2 * D_ATT + 2 * D_ATT + 2 * 2 * D_MODEL + 4 * D_PLE + 4 * D_MODEL)
    scratch_bytes = tm * (6 * 4 * D_MODEL + 6 * FF_CHUNK)
    return pl.pallas_call(
        _post_kernel,
        grid=(t // tm,),
        in_specs=[row(D_MODEL), row(D_ATT), row(D_ATT), row(2 * D_MODEL), row(D_PLE)]
                 + [_const_spec(c.shape) for c in consts],
        out_specs=row(D_MODEL),
        out_shape=jax.ShapeDtypeStruct((t, D_MODEL), F32),
        compiler_params=pltpu.CompilerParams(
            dimension_semantics=("arbitrary",),
            vmem_limit_bytes=w_bytes + 2 * tile_bytes + scratch_bytes),
        name="post",
    )(x2, o_fox, o_sb, gl, p2, *consts)


def kernel(x, p, g_mix, w_in, b_forget, b_gate, w_branch_fox, w_branch_sb, w_out,
           g_mlp, w_up, w_down, g_ple, w_ple_gate, w_ple, g_final):
    b, s_len, d = x.shape
    t = b * s_len
    depth = w_in.shape[0]
    assert depth == 1, "the post kernel fuses the final RMSNorm into the (single) layer"
    x2 = x.reshape(t, d)
    for i in range(depth):
        o0, o1, o2 = 3 * D_ATT, 3 * D_ATT + N_HEADS, 6 * D_ATT + N_HEADS
        wa, wf, wb, wg = (w_in[i][:, lo:hi].astype(BF16) for lo, hi in ((0, o0), (o0, o1), (o1, o2), (o2, None)))
        wf = jnp.pad(wf, ((0, 0), (0, LANES - N_HEADS)))
        b_pad = jnp.pad(b_forget[i], (0, LANES - N_HEADS)).reshape(1, LANES)

        qkv_fox, f_log, qkv_sb, gl = _in_proj(x2, g_mix[i].reshape(1, d), wa, wf, wb, wg)
        ctok, ct = _forget_cumsum(f_log.reshape(b, s_len, LANES), b_pad)
        o_fox, o_sb = _attention(qkv_fox.reshape(b, s_len, 3 * D_ATT), qkv_sb.reshape(b, s_len, 3 * D_ATT),
                                 ctok, ct)

        consts = (b_gate[i], w_branch_fox[i].astype(BF16), w_branch_sb[i].astype(BF16),
                  w_out[i].astype(BF16), g_mlp[i].reshape(1, d), w_up[i].astype(BF16),
                  w_down[i].astype(BF16), g_ple[i].reshape(1, d), w_ple_gate[i].astype(BF16),
                  w_ple[i].astype(BF16), g_final.reshape(1, d))
        x2 = _post(x2, o_fox.reshape(t, D_ATT), o_sb.reshape(t, D_ATT), gl,
                   p[i].reshape(t, D_PLE), consts)
    return x2.reshape(b, s_len, d)
```

```python
import functools

import jax
import jax.numpy as jnp
import numpy as np
from jax import lax
from jax.experimental import pallas as pl
from jax.experimental.pallas import tpu as pltpu

D_MODEL = 1024
HEAD_DIM = 64
N_HEADS = 8
D_ATT = N_HEADS * HEAD_DIM
D_FF = 4 * D_MODEL
D_PLE = 256
EPS = 1e-6
SCALE = HEAD_DIM ** -0.5
LOG2E = 1.4426950408889634

LANES = 128
SUBLANES = 8
HEADS_PER_STEP = 4
PAIRS_PER_STEP = HEADS_PER_STEP // 2
PAIR_LANES = 2 * HEAD_DIM
STEP_LANES = HEADS_PER_STEP * HEAD_DIM
NEG_BIG = -1e30
ONES_ROWS = 16
SB_DEAD_CARRY = 90.0

TM_PROJ = 512
TM_POST = 512
FF_CHUNK = 512
TQ = 256
TK = 256
HALF = TK // 2
CUM_BLK = 128

BF16 = jnp.bfloat16
F32 = jnp.float32

_NT = (((1,), (1,)), ((), ()))


def _const_spec(shape):
    return pl.BlockSpec(shape, lambda *_: (0,) * len(shape), pipeline_mode=pl.Buffered(1))


def _rms_scale(x, g):
    r = lax.rsqrt(jnp.mean(x * x, axis=-1, keepdims=True) + EPS)
    return x * r * g


def _in_proj_kernel(x_ref, g_ref, wa_ref, wf_ref, wb_ref, wg_ref,
                    qkva_ref, f_ref, qkvb_ref, gl_ref):
    h = _rms_scale(x_ref[...], g_ref[...]).astype(BF16)
    for w_ref, out_ref, q_scale in ((wa_ref, qkva_ref, SCALE * LOG2E), (wb_ref, qkvb_ref, SCALE)):
        qkv = jnp.dot(h, w_ref[...], preferred_element_type=F32)
        out_ref[:, :D_ATT] = (qkv[:, :D_ATT] * q_scale).astype(BF16)
        out_ref[:, D_ATT:] = qkv[:, D_ATT:].astype(BF16)
    f_ref[...] = jnp.dot(h, wf_ref[...], preferred_element_type=F32)
    gl_ref[...] = jnp.dot(h, wg_ref[...], preferred_element_type=F32).astype(BF16)


def _in_proj(x2, g_mix, wa, wf, wb, wg):
    t = x2.shape[0]
    tm = TM_PROJ
    row = lambda n: pl.BlockSpec((tm, n), lambda i: (i, 0))
    w_bytes = 2 * D_MODEL * (wa.shape[1] + wf.shape[1] + wb.shape[1] + wg.shape[1])
    tile_bytes = tm * (4 * D_MODEL + 2 * wa.shape[1] + 4 * wf.shape[1] + 2 * wb.shape[1] + 2 * wg.shape[1])
    scratch_bytes = 4 * tm * wg.shape[1] + 2 * tm * D_MODEL
    return pl.pallas_call(
        _in_proj_kernel,
        grid=(t // tm,),
        in_specs=[row(D_MODEL), _const_spec(g_mix.shape), _const_spec(wa.shape), _const_spec(wf.shape),
                  _const_spec(wb.shape), _const_spec(wg.shape)],
        out_specs=[row(wa.shape[1]), row(wf.shape[1]), row(wb.shape[1]), row(wg.shape[1])],
        out_shape=[jax.ShapeDtypeStruct((t, wa.shape[1]), BF16),
                   jax.ShapeDtypeStruct((t, wf.shape[1]), F32),
                   jax.ShapeDtypeStruct((t, wb.shape[1]), BF16),
                   jax.ShapeDtypeStruct((t, wg.shape[1]), BF16)],
        compiler_params=pltpu.CompilerParams(
            dimension_semantics=("arbitrary",),
            vmem_limit_bytes=w_bytes + 2 * tile_bytes + 2 * scratch_bytes),
        name="in_proj",
    )(x2, g_mix, wa, wf, wb, wg)


def _split3(a):
    a1 = a.astype(BF16)
    r1 = a - a1.astype(F32)
    a2 = r1.astype(BF16)
    a3 = (r1 - a2.astype(F32)).astype(BF16)
    return a1, a2, a3


def _forget_cumsum_kernel(f_ref, b_ref, ctok_ref, ct_ref):
    s_len = f_ref.shape[1]
    row = lax.broadcasted_iota(jnp.int32, (CUM_BLK, CUM_BLK), 0)
    col = lax.broadcasted_iota(jnp.int32, (CUM_BLK, CUM_BLK), 1)
    tri = (col <= row).astype(BF16)
    carry = jnp.zeros((1, LANES), F32)
    for blk in range(s_len // CUM_BLK):
        rows = pl.ds(blk * CUM_BLK, CUM_BLK)
        y = f_ref[0, rows, :] + b_ref[...]
        log_f = -(jnp.maximum(-y, 0.0) + jnp.log(1.0 + jnp.exp(-jnp.abs(y))))
        c = carry
        for part in _split3(log_f):
            c = c + jnp.dot(tri, part, preferred_element_type=F32)
        carry = c[CUM_BLK - 1:CUM_BLK, :]
        c2 = c * LOG2E
        ctok_ref[0, rows, :] = c2
        ct_ref[0, :, rows] = jnp.transpose(c2)[:SUBLANES, :]


def _forget_cumsum(f3, b_pad):
    b, s_len, _ = f3.shape
    return pl.pallas_call(
        _forget_cumsum_kernel,
        grid=(b,),
        in_specs=[pl.BlockSpec((1, s_len, LANES), lambda i: (i, 0, 0)), _const_spec(b_pad.shape)],
        out_specs=[pl.BlockSpec((1, s_len, LANES), lambda i: (i, 0, 0)),
                   pl.BlockSpec((1, SUBLANES, s_len), lambda i: (i, 0, 0))],
        out_shape=[jax.ShapeDtypeStruct((b, s_len, LANES), F32),
                   jax.ShapeDtypeStruct((b, SUBLANES, s_len), F32)],
        compiler_params=pltpu.CompilerParams(dimension_semantics=("arbitrary",)),
        name="forget_cumsum",
    )(f3, b_pad)


def _head_mask():
    lane = np.arange(STEP_LANES)[None, :] // HEAD_DIM
    head = np.repeat(np.arange(HEADS_PER_STEP), TQ)[:, None]
    return jnp.asarray((lane == head).astype(np.float32), BF16)


def _pair_mask():
    lane = np.arange(PAIR_LANES)[None, :] // HEAD_DIM
    half = np.repeat(np.arange(2), TK)[:, None]
    return jnp.asarray((lane == half).astype(np.float32), BF16)


def _causal_bias(strict, keys_on_rows):
    key_pos, q_pos = np.arange(TK)[:, None], np.arange(TQ)[None, :]
    allowed = (key_pos < q_pos) if strict else (key_pos <= q_pos)
    bias = np.where(allowed, 0.0, NEG_BIG).astype(np.float32)
    return jnp.asarray(bias if keys_on_rows else bias.T, F32)


def _stack_masked_queries(q_ref, head_mask_ref, qs_ref):
    qs_ref[...] = jnp.concatenate([q_ref[0]] * HEADS_PER_STEP, axis=0) * head_mask_ref[...]


def _key_rows(j):
    return pl.ds(pl.multiple_of(j * TK, TK), TK)


_HALVES = tuple(slice(i * PAIRS_PER_STEP * TQ, (i + 1) * PAIRS_PER_STEP * TQ) for i in range(2))


def _fox_stages(group, qi, k_ref, ct_ref, causal_t_ref, qs_ref, vt_ref, ckb_ref, s_ref, p_ref, acc_ref):
    sub = lax.broadcasted_iota(jnp.int32, (SUBLANES, TQ), 0)
    ones_rows = jnp.ones((ONES_ROWS, TK), BF16)
    c8 = ct_ref[0, :, pl.ds(pl.multiple_of(qi * TQ, TQ), TQ)]
    c_q = [jnp.sum(jnp.where(sub == HEADS_PER_STEP * group + h, c8, 0.0), axis=0, keepdims=True)
           for h in range(HEADS_PER_STEP)]

    def scores(j, s_ref=s_ref):
        k4 = k_ref[0, _key_rows(j), :]
        for h in range(HEADS_PER_STEP):
            s_ref[h] = lax.dot_general(k4, qs_ref[h * TQ:(h + 1) * TQ, :], _NT,
                                       preferred_element_type=F32)

    def values(j, alphas, p_ref=p_ref):
        for h in range(HEADS_PER_STEP):
            lhs = jnp.concatenate([vt_ref[j, h * HEAD_DIM:(h + 1) * HEAD_DIM, :], ones_rows], axis=0)
            acc_ref[h] = alphas[h] * acc_ref[h] + jnp.dot(lhs, p_ref[h], preferred_element_type=F32)

    def softmax_block(j, m_run, s_ref=s_ref, p_ref=p_ref):
        m_next, alphas = [], []
        for h in range(HEADS_PER_STEP):
            c_k = ckb_ref[h, _key_rows(j), :]
            u = s_ref[h] - jnp.concatenate([c_k, c_k], axis=1)
            m_new = jnp.maximum(m_run[h], jnp.max(u, axis=0, keepdims=True) + c_q[h])
            alphas.append(jnp.exp2(m_run[h] - m_new))
            p_ref[h] = jnp.exp2(u - (m_new - c_q[h])).astype(BF16)
            m_next.append(m_new)
        return tuple(m_next), tuple(alphas)

    def softmax_diagonal(m_run):
        tri = causal_t_ref[:HALF, :HALF]
        m_next, alphas = [], []
        for h in range(HEADS_PER_STEP):
            c_k = ckb_ref[h, _key_rows(qi), :]
            c_top, c_bot = c_k[:HALF], c_k[HALF:]
            u_tl = s_ref[h, :HALF, :HALF] - c_top + tri
            u_tr = s_ref[h, :HALF, HALF:] - c_top
            u_br = s_ref[h, HALF:, HALF:] - c_bot + tri
            col_max = jnp.concatenate(
                [jnp.max(u_tl, axis=0, keepdims=True),
                 jnp.maximum(jnp.max(u_tr, axis=0, keepdims=True), jnp.max(u_br, axis=0, keepdims=True))], axis=1)
            m_new = jnp.maximum(m_run[h], col_max + c_q[h])
            alphas.append(jnp.exp2(m_run[h] - m_new))
            shift = m_new - c_q[h]
            p_ref[h, :HALF, :HALF] = jnp.exp2(u_tl - shift[:, :HALF]).astype(BF16)
            p_ref[h, :HALF, HALF:] = jnp.exp2(u_tr - shift[:, HALF:]).astype(BF16)
            p_ref[h, HALF:, :HALF] = jnp.zeros((HALF, HALF), BF16)
            p_ref[h, HALF:, HALF:] = jnp.exp2(u_br - shift[:, HALF:]).astype(BF16)
            m_next.append(m_new)
        return tuple(m_next), tuple(alphas)

    return scores, values, softmax_block, softmax_diagonal


def _sb_stages(k_ref, v_ref, pair_mask_ref, causal_ref, tri_ref, qs_ref, z_ref, w_ref, acc_ref):
    pair_mask = pair_mask_ref[...]

    def scores(j):
        k4 = k_ref[0, _key_rows(j), :]
        for half in _HALVES:
            z_ref[half, :] = lax.dot_general(qs_ref[half, :], k4, _NT, preferred_element_type=F32)

    def values(j):
        v4 = v_ref[0, _key_rows(j), :]
        for p in range(PAIRS_PER_STEP):
            v_pair = v4[:, p * PAIR_LANES:(p + 1) * PAIR_LANES]
            vbd = jnp.concatenate([v_pair, v_pair], axis=0) * pair_mask
            acc_ref[p] += jnp.dot(w_ref[p], vbd, preferred_element_type=F32)

    def log_beta_and_softplus(z):
        log_beta = jnp.minimum(z, 0.0) - jnp.log(1.0 + jnp.exp2(jnp.abs(z) * -LOG2E))
        return log_beta, (z - log_beta).astype(BF16)

    def suffix_sums(sps):
        csums = [jnp.dot(jnp.concatenate(sps[2 * p:2 * p + 2], axis=0), tri_ref[...],
                         preferred_element_type=F32) for p in range(PAIRS_PER_STEP)]
        return [csums[h // 2][(h % 2) * TQ:(h % 2 + 1) * TQ] for h in range(HEADS_PER_STEP)]

    def extend(carry, csums, sps):
        return tuple(carry[h] + (csums[h][:, 0:1] + sps[h][:, 0:1].astype(F32)) for h in range(HEADS_PER_STEP))

    def weights(carry):
        log_betas, sps = zip(*(log_beta_and_softplus(z_ref[h * TQ:(h + 1) * TQ, :])
                               for h in range(HEADS_PER_STEP)))
        csums = suffix_sums(sps)
        for h in range(HEADS_PER_STEP):
            w_ref[h // 2, :, (h % 2) * TK:(h % 2 + 1) * TK] = (
                jnp.exp(log_betas[h] - csums[h] - carry[h]).astype(BF16))
        return extend(carry, csums, sps)

    def weights_diagonal(carry):
        tri = causal_ref[:HALF, :HALF]
        rows = lambda h, lo: slice(h * TQ + lo, h * TQ + lo + HALF)
        lb_tl, lb_bot, sps = [], [], []
        for h in range(HEADS_PER_STEP):
            lb, sp_tl = log_beta_and_softplus(z_ref[rows(h, 0), :HALF] + tri)
            lb_tl.append(lb)
            lb_bl, sp_bl = log_beta_and_softplus(z_ref[rows(h, HALF), :HALF])
            lb_br, sp_br = log_beta_and_softplus(z_ref[rows(h, HALF), HALF:] + tri)
            lb_bot.append(jnp.concatenate([lb_bl, lb_br], axis=1))
            sps.append(jnp.concatenate([jnp.concatenate([sp_tl, jnp.zeros_like(sp_tl)], axis=1),
                                        jnp.concatenate([sp_bl, sp_br], axis=1)], axis=0))
        csums = suffix_sums(sps)
        for h in range(HEADS_PER_STEP):
            lanes = slice((h % 2) * TK, (h % 2 + 1) * TK)
            left = slice((h % 2) * TK, (h % 2) * TK + HALF)
            right = slice((h % 2) * TK + HALF, (h % 2 + 1) * TK)
            w_ref[h // 2, :HALF, left] = jnp.exp(lb_tl[h] - csums[h][:HALF, :HALF] - carry[h][:HALF]).astype(BF16)
            w_ref[h // 2, :HALF, right] = jnp.zeros((HALF, HALF), BF16)
            w_ref[h // 2, HALF:, lanes] = jnp.exp(lb_bot[h] - csums[h][HALF:] - carry[h][HALF:]).astype(BF16)
        return extend(carry, csums, sps)

    return scores, values, weights, weights_diagonal


def _attn_kernel(fq_ref, fk_ref, fv_ref, ctok_ref, ct_ref, sq_ref, sk_ref, sv_ref,
                 head_mask_ref, fox_causal_ref, pair_mask_ref, sb_causal_ref, tri_ref,
                 fo_ref, so_ref,
                 fqs_ref, vt_ref, ckb_ref, s_ref, p_ref, s1_ref, p1_ref, facc_ref, sqs_ref, z_ref, w_ref, sacc_ref):
    group = pl.program_id(1)
    qi = pl.program_id(2)

    @pl.when(qi == 0)
    def _():
        lane = lax.broadcasted_iota(jnp.int32, (TK, LANES), 1)
        for j in range(fk_ref.shape[1] // TK):
            rows = slice(j * TK, (j + 1) * TK)
            vt_ref[j] = jnp.transpose(fv_ref[0, rows, :].astype(F32)).astype(BF16)
            ctok = ctok_ref[0, rows, :]
            for h in range(HEADS_PER_STEP):
                c_col = jnp.sum(jnp.where(lane == HEADS_PER_STEP * group + h, ctok, 0.0),
                                axis=-1, keepdims=True)
                ckb_ref[h, rows, :] = jnp.broadcast_to(c_col, (TK, LANES))

    _stack_masked_queries(fq_ref, head_mask_ref, fqs_ref)
    _stack_masked_queries(sq_ref, head_mask_ref, sqs_ref)
    facc_ref[...] = jnp.zeros(facc_ref.shape, F32)
    sacc_ref[...] = jnp.zeros(sacc_ref.shape, F32)
    fox_scores, fox_values, fox_softmax, fox_softmax_diagonal = _fox_stages(
        group, qi, fk_ref, ct_ref, fox_causal_ref, fqs_ref, vt_ref, ckb_ref, s_ref, p_ref, facc_ref)
    sb_scores, sb_values, sb_weights, sb_weights_diagonal = _sb_stages(
        sk_ref, sv_ref, pair_mask_ref, sb_causal_ref, tri_ref, sqs_ref, z_ref, w_ref, sacc_ref)

    def still_live(carry):
        lowest = functools.reduce(jnp.minimum, carry)
        return jnp.min(lowest) < SB_DEAD_CARRY

    fox_scores(qi)
    sb_scores(qi)
    m_init = tuple(jnp.full((1, TQ), NEG_BIG, F32) for _ in range(HEADS_PER_STEP))
    fox_state = fox_softmax_diagonal(m_init)
    carry = sb_weights_diagonal(tuple(jnp.zeros((TQ, 1), F32) for _ in range(HEADS_PER_STEP)))
    nxt = jnp.maximum(qi - 1, 0)
    fox_scores(nxt)
    sb_scores(nxt)

    def fox_step(t, state):
        m_run, alphas = state
        fox_values(qi - t + 1, alphas)
        state = fox_softmax(qi - t, m_run)
        fox_scores(jnp.maximum(qi - t - 1, 0))
        return state

    def both_step(state):
        t, (m_run, alphas), carry, _ = state
        fox_values(qi - t + 1, alphas)
        sb_values(qi - t + 1)
        fox_state = fox_softmax(qi - t, m_run)
        carry = sb_weights(carry)
        nxt = jnp.maximum(qi - t - 1, 0)
        fox_scores(nxt)
        sb_scores(nxt)
        return t + 1, fox_state, carry, still_live(carry)

    t_sb, fox_state, _, _ = lax.while_loop(lambda st: (st[0] <= qi) & st[3], both_step,
                                           (jnp.int32(1), fox_state, carry, jnp.bool_(True)))

    def fox_two_steps(i, state):
        t = t_sb + 2 * i
        m_run, alphas = state
        fox_values(qi - t + 1, alphas)
        m_run, alphas = fox_softmax(qi - t, m_run, s_ref, p1_ref)
        fox_scores(jnp.maximum(qi - t - 1, 0), s1_ref)
        fox_values(qi - t, alphas, p1_ref)
        state = fox_softmax(qi - t - 1, m_run, s1_ref, p_ref)
        fox_scores(jnp.maximum(qi - t - 2, 0))
        return state

    pairs = (qi + 1 - t_sb) // 2
    fox_state = lax.fori_loop(0, pairs, fox_two_steps, fox_state)
    _, alphas = lax.fori_loop(t_sb + 2 * pairs, qi + 1, fox_step, fox_state)

    fox_values(0, alphas)
    sb_values(qi - t_sb + 1)
    out_t = jnp.concatenate([facc_ref[h, :HEAD_DIM, :] / facc_ref[h, HEAD_DIM:HEAD_DIM + 1, :]
                             for h in range(HEADS_PER_STEP)], axis=0)
    fo_ref[0] = jnp.transpose(out_t).astype(BF16)
    so_ref[0] = jnp.concatenate([sacc_ref[p] for p in range(PAIRS_PER_STEP)], axis=1).astype(BF16)


def _attention(qkv_fox, qkv_sb, ctok, ct):
    b, s_len, _ = qkv_fox.shape
    groups = D_ATT // STEP_LANES
    q_spec = pl.BlockSpec((1, TQ, STEP_LANES), lambda b, g, i: (b, i, g))
    k_spec = pl.BlockSpec((1, s_len, STEP_LANES), lambda b, g, i: (b, 0, groups + g))
    v_spec = pl.BlockSpec((1, s_len, STEP_LANES), lambda b, g, i: (b, 0, 2 * groups + g))
    o_spec = pl.BlockSpec((1, TQ, STEP_LANES), lambda b, g, i: (b, i, g))
    tri = jnp.asarray(np.tril(np.ones((TK, TK), np.float32), -1), BF16)
    consts = (_head_mask(), _causal_bias(strict=False, keys_on_rows=True), _pair_mask(),
              _causal_bias(strict=True, keys_on_rows=False), tri)
    out = jax.ShapeDtypeStruct((b, s_len, D_ATT), BF16)
    return pl.pallas_call(
        _attn_kernel,
        grid=(b, groups, s_len // TQ),
        in_specs=[q_spec, k_spec, v_spec,
                  pl.BlockSpec((1, s_len, LANES), lambda b, g, i: (b, 0, 0)),
                  pl.BlockSpec((1, SUBLANES, s_len), lambda b, g, i: (b, 0, 0)),
                  q_spec, k_spec, v_spec] + [_const_spec(c.shape) for c in consts],
        out_specs=[o_spec, o_spec],
        out_shape=[out, out],
        scratch_shapes=[pltpu.VMEM((HEADS_PER_STEP * TQ, STEP_LANES), BF16),
                        pltpu.VMEM((s_len // TK, STEP_LANES, TK), BF16),
                        pltpu.VMEM((HEADS_PER_STEP, s_len, LANES), F32),
                        pltpu.VMEM((HEADS_PER_STEP, TK, TQ), F32),
                        pltpu.VMEM((HEADS_PER_STEP, TK, TQ), BF16),
                        pltpu.VMEM((HEADS_PER_STEP, TK, TQ), F32),
                        pltpu.VMEM((HEADS_PER_STEP, TK, TQ), BF16),
                        pltpu.VMEM((HEADS_PER_STEP, HEAD_DIM + ONES_ROWS, TQ), F32),
                        pltpu.VMEM((HEADS_PER_STEP * TQ, STEP_LANES), BF16),
                        pltpu.VMEM((HEADS_PER_STEP * TQ, TK), F32),
                        pltpu.VMEM((PAIRS_PER_STEP, TQ, 2 * TK), BF16),
                        pltpu.VMEM((PAIRS_PER_STEP, TQ, PAIR_LANES), F32)],
        compiler_params=pltpu.CompilerParams(dimension_semantics=("arbitrary",) * 3),
        name="attention",
    )(qkv_fox, qkv_fox, qkv_fox, ctok, ct, qkv_sb, qkv_sb, qkv_sb, *consts)


def _post_kernel(x_ref, of_ref, os_ref, gl_ref, p_ref, bg_ref, wbf_ref, wbs_ref, wout_ref,
                 gmlp_ref, wup_ref, wdown_ref, gple_ref, wpg_ref, wple_ref, gfin_ref, out_ref):
    o_fox = jnp.dot(of_ref[...], wbf_ref[...], preferred_element_type=F32)
    o_sb = jnp.dot(os_ref[...], wbs_ref[...], preferred_element_type=F32)
    gate_a = jax.nn.sigmoid(gl_ref[:, :D_MODEL].astype(F32) + bg_ref[0:1, :])
    gate_b = jax.nn.sigmoid(gl_ref[:, D_MODEL:].astype(F32) + bg_ref[1:2, :])
    merged = (gate_a * o_fox + gate_b * o_sb).astype(BF16)
    x1 = x_ref[...] + jnp.dot(merged, wout_ref[...], preferred_element_type=F32)

    h = _rms_scale(x1, gmlp_ref[...]).astype(BF16)
    mlp = jnp.zeros_like(x1)
    for c in range(D_FF // FF_CHUNK):
        cols = slice(c * FF_CHUNK, (c + 1) * FF_CHUNK)
        up = jnp.maximum(jnp.dot(h, wup_ref[:, cols], preferred_element_type=F32), 0.0)
        mlp = mlp + jnp.dot((up * up).astype(BF16), wdown_ref[cols, :], preferred_element_type=F32)
    x2 = x1 + mlp

    h = _rms_scale(x2, gple_ref[...]).astype(BF16)
    gate = jax.nn.sigmoid(jnp.dot(h, wpg_ref[...], preferred_element_type=F32))
    emb = jnp.dot(p_ref[...].astype(BF16), wple_ref[...], preferred_element_type=F32)
    x3 = x2 + gate * emb
    out_ref[...] = _rms_scale(x3, gfin_ref[...])


def _post(x2, o_fox, o_sb, gl, p2, consts):
    t = x2.shape[0]
    tm = TM_POST
    row = lambda n: pl.BlockSpec((tm, n), lambda i: (i, 0))
    w_bytes = sum(int(c.size) * c.dtype.itemsize for c in consts)
    tile_bytes = tm * (4 * D_MODEL + 2 * D_ATT + 2 * D_ATT + 2 * 2 * D_MODEL + 4 * D_PLE + 4 * D_MODEL)
    scratch_bytes = tm * (6 * 4 * D_MODEL + 6 * FF_CHUNK)
    return pl.pallas_call(
        _post_kernel,
        grid=(t // tm,),
        in_specs=[row(D_MODEL), row(D_ATT), row(D_ATT), row(2 * D_MODEL), row(D_PLE)]
                 + [_const_spec(c.shape) for c in consts],
        out_specs=row(D_MODEL),
        out_shape=jax.ShapeDtypeStruct((t, D_MODEL), F32),
        compiler_params=pltpu.CompilerParams(
            dimension_semantics=("arbitrary",),
            vmem_limit_bytes=w_bytes + 2 * tile_bytes + scratch_bytes),
        name="post",
    )(x2, o_fox, o_sb, gl, p2, *consts)


def kernel(x, p, g_mix, w_in, b_forget, b_gate, w_branch_fox, w_branch_sb, w_out,
           g_mlp, w_up, w_down, g_ple, w_ple_gate, w_ple, g_final):
    b, s_len, d = x.shape
    t = b * s_len
    depth = w_in.shape[0]
    assert depth == 1, "the post kernel fuses the final RMSNorm into the (single) layer"
    x2 = x.reshape(t, d)
    for i in range(depth):
        o0, o1, o2 = 3 * D_ATT, 3 * D_ATT + N_HEADS, 6 * D_ATT + N_HEADS
        wa, wf, wb, wg = (w_in[i][:, lo:hi].astype(BF16) for lo, hi in ((0, o0), (o0, o1), (o1, o2), (o2, None)))
        wf = jnp.pad(wf, ((0, 0), (0, LANES - N_HEADS)))
        b_pad = jnp.pad(b_forget[i], (0, LANES - N_HEADS)).reshape(1, LANES)

        qkv_fox, f_log, qkv_sb, gl = _in_proj(x2, g_mix[i].reshape(1, d), wa, wf, wb, wg)
        ctok, ct = _forget_cumsum(f_log.reshape(b, s_len, LANES), b_pad)
        o_fox, o_sb = _attention(qkv_fox.reshape(b, s_len, 3 * D_ATT), qkv_sb.reshape(b, s_len, 3 * D_ATT),
                                 ctok, ct)

        consts = (b_gate[i], w_branch_fox[i].astype(BF16), w_branch_sb[i].astype(BF16),
                  w_out[i].astype(BF16), g_mlp[i].reshape(1, d), w_up[i].astype(BF16),
                  w_down[i].astype(BF16), g_ple[i].reshape(1, d), w_ple_gate[i].astype(BF16),
                  w_ple[i].astype(BF16), g_final.reshape(1, d))
        x2 = _post(x2, o_fox.reshape(t, D_ATT), o_sb.reshape(t, D_ATT), gl,
                   p[i].reshape(t, D_PLE), consts)
    return x2.reshape(b, s_len, d)
```

```python
import functools

import jax
import jax.numpy as jnp
import numpy as np
from jax import lax
from jax.experimental import pallas as pl
from jax.experimental.pallas import tpu as pltpu

D_MODEL = 1024
HEAD_DIM = 64
N_HEADS = 8
D_ATT = N_HEADS * HEAD_DIM
D_FF = 4 * D_MODEL
D_PLE = 256
EPS = 1e-6
SCALE = HEAD_DIM ** -0.5
LOG2E = 1.4426950408889634

LANES = 128
SUBLANES = 8
HEADS_PER_STEP = 4
PAIRS_PER_STEP = HEADS_PER_STEP // 2
PAIR_LANES = 2 * HEAD_DIM
STEP_LANES = HEADS_PER_STEP * HEAD_DIM
NEG_BIG = -1e30
ONES_ROWS = 16
SB_DEAD_CARRY = 90.0

TM_PROJ = 512
TM_POST = 512
FF_CHUNK = 512
TQ = 256
TK = 256
HALF = TK // 2
CUM_BLK = 128

BF16 = jnp.bfloat16
F32 = jnp.float32

_NT = (((1,), (1,)), ((), ()))


def _const_spec(shape):
    return pl.BlockSpec(shape, lambda *_: (0,) * len(shape), pipeline_mode=pl.Buffered(1))


def _rms_scale(x, g):
    r = lax.rsqrt(jnp.mean(x * x, axis=-1, keepdims=True) + EPS)
    return x * r * g


_COL_F = 3 * D_ATT
_COL_B = _COL_F + N_HEADS
_COL_G = _COL_B + 3 * D_ATT
W_ROWS = 256


def _cast_w_in_kernel(w_ref, wa_ref, wf_ref, wb_ref, wg_ref):
    wa_ref[...] = w_ref[:, :_COL_F].astype(BF16)
    lane = lax.broadcasted_iota(jnp.int32, (W_ROWS, LANES), 1)
    wf_ref[...] = jnp.where(lane < N_HEADS, w_ref[:, _COL_F:_COL_F + LANES], 0.0).astype(BF16)
    wb_ref[...] = w_ref[:, _COL_B:_COL_G].astype(BF16)
    wg_ref[...] = w_ref[:, _COL_G:].astype(BF16)


def _cast_w_in(w):
    k, n = w.shape
    widths = (_COL_F, LANES, _COL_G - _COL_B, n - _COL_G)
    return pl.pallas_call(
        _cast_w_in_kernel,
        grid=(k // W_ROWS,),
        in_specs=[pl.BlockSpec((W_ROWS, n), lambda i: (i, 0))],
        out_specs=[pl.BlockSpec((W_ROWS, c), lambda i: (i, 0)) for c in widths],
        out_shape=[jax.ShapeDtypeStruct((k, c), BF16) for c in widths],
        compiler_params=pltpu.CompilerParams(dimension_semantics=("arbitrary",)),
        name="cast_w_in",
    )(w)


def _in_proj_kernel(x_ref, g_ref, wa_ref, wf_ref, wb_ref, wg_ref,
                    qkva_ref, f_ref, qkvb_ref, gl_ref):
    h = _rms_scale(x_ref[...], g_ref[...]).astype(BF16)
    for w_ref, out_ref, q_scale in ((wa_ref, qkva_ref, SCALE * LOG2E), (wb_ref, qkvb_ref, SCALE)):
        qkv = jnp.dot(h, w_ref[...], preferred_element_type=F32)
        out_ref[:, :D_ATT] = (qkv[:, :D_ATT] * q_scale).astype(BF16)
        out_ref[:, D_ATT:] = qkv[:, D_ATT:].astype(BF16)
    f_ref[...] = jnp.dot(h, wf_ref[...], preferred_element_type=F32)
    gl_ref[...] = jnp.dot(h, wg_ref[...], preferred_element_type=F32).astype(BF16)


def _in_proj(x2, g_mix, wa, wf, wb, wg):
    t = x2.shape[0]
    tm = TM_PROJ
    row = lambda n: pl.BlockSpec((tm, n), lambda i: (i, 0))
    w_bytes = 2 * D_MODEL * (wa.shape[1] + wf.shape[1] + wb.shape[1] + wg.shape[1])
    tile_bytes = tm * (4 * D_MODEL + 2 * wa.shape[1] + 4 * wf.shape[1] + 2 * wb.shape[1] + 2 * wg.shape[1])
    scratch_bytes = 4 * tm * wg.shape[1] + 2 * tm * D_MODEL
    return pl.pallas_call(
        _in_proj_kernel,
        grid=(t // tm,),
        in_specs=[row(D_MODEL), _const_spec(g_mix.shape), _const_spec(wa.shape), _const_spec(wf.shape),
                  _const_spec(wb.shape), _const_spec(wg.shape)],
        out_specs=[row(wa.shape[1]), row(wf.shape[1]), row(wb.shape[1]), row(wg.shape[1])],
        out_shape=[jax.ShapeDtypeStruct((t, wa.shape[1]), BF16),
                   jax.ShapeDtypeStruct((t, wf.shape[1]), F32),
                   jax.ShapeDtypeStruct((t, wb.shape[1]), BF16),
                   jax.ShapeDtypeStruct((t, wg.shape[1]), BF16)],
        compiler_params=pltpu.CompilerParams(
            dimension_semantics=("arbitrary",),
            vmem_limit_bytes=w_bytes + 2 * tile_bytes + 2 * scratch_bytes),
        name="in_proj",
    )(x2, g_mix, wa, wf, wb, wg)


def _split3(a):
    a1 = a.astype(BF16)
    r1 = a - a1.astype(F32)
    a2 = r1.astype(BF16)
    a3 = (r1 - a2.astype(F32)).astype(BF16)
    return a1, a2, a3


def _forget_cumsum_kernel(f_ref, b_ref, ctok_ref, ct_ref):
    s_len = f_ref.shape[1]
    row = lax.broadcasted_iota(jnp.int32, (CUM_BLK, CUM_BLK), 0)
    col = lax.broadcasted_iota(jnp.int32, (CUM_BLK, CUM_BLK), 1)
    tri = (col <= row).astype(BF16)
    carry = jnp.zeros((1, LANES), F32)
    for blk in range(s_len // CUM_BLK):
        rows = pl.ds(blk * CUM_BLK, CUM_BLK)
        y = f_ref[0, rows, :] + b_ref[...]
        log_f = -(jnp.maximum(-y, 0.0) + jnp.log(1.0 + jnp.exp(-jnp.abs(y))))
        c = carry
        for part in _split3(log_f):
            c = c + jnp.dot(tri, part, preferred_element_type=F32)
        carry = c[CUM_BLK - 1:CUM_BLK, :]
        c2 = c * LOG2E
        ctok_ref[0, rows, :] = c2
        ct_ref[0, :, rows] = jnp.transpose(c2)[:SUBLANES, :]


def _forget_cumsum(f3, b_pad):
    b, s_len, _ = f3.shape
    return pl.pallas_call(
        _forget_cumsum_kernel,
        grid=(b,),
        in_specs=[pl.BlockSpec((1, s_len, LANES), lambda i: (i, 0, 0)), _const_spec(b_pad.shape)],
        out_specs=[pl.BlockSpec((1, s_len, LANES), lambda i: (i, 0, 0)),
                   pl.BlockSpec((1, SUBLANES, s_len), lambda i: (i, 0, 0))],
        out_shape=[jax.ShapeDtypeStruct((b, s_len, LANES), F32),
                   jax.ShapeDtypeStruct((b, SUBLANES, s_len), F32)],
        compiler_params=pltpu.CompilerParams(dimension_semantics=("arbitrary",)),
        name="forget_cumsum",
    )(f3, b_pad)


def _head_mask():
    lane = np.arange(STEP_LANES)[None, :] // HEAD_DIM
    head = np.repeat(np.arange(HEADS_PER_STEP), TQ)[:, None]
    return jnp.asarray((lane == head).astype(np.float32), BF16)


def _pair_mask():
    lane = np.arange(PAIR_LANES)[None, :] // HEAD_DIM
    half = np.repeat(np.arange(2), TK)[:, None]
    return jnp.asarray((lane == half).astype(np.float32), BF16)


def _causal_bias(strict, keys_on_rows):
    key_pos, q_pos = np.arange(TK)[:, None], np.arange(TQ)[None, :]
    allowed = (key_pos < q_pos) if strict else (key_pos <= q_pos)
    bias = np.where(allowed, 0.0, NEG_BIG).astype(np.float32)
    return jnp.asarray(bias if keys_on_rows else bias.T, F32)


def _stack_masked_queries(q_ref, head_mask_ref, qs_ref):
    qs_ref[...] = jnp.concatenate([q_ref[0]] * HEADS_PER_STEP, axis=0) * head_mask_ref[...]


def _key_rows(j):
    return pl.ds(pl.multiple_of(j * TK, TK), TK)


_HALVES = tuple(slice(i * PAIRS_PER_STEP * TQ, (i + 1) * PAIRS_PER_STEP * TQ) for i in range(2))


def _fox_stages(group, qi, k_ref, ct_ref, causal_t_ref, qs_ref, vt_ref, ckb_ref, s_ref, p_ref, acc_ref):
    sub = lax.broadcasted_iota(jnp.int32, (SUBLANES, TQ), 0)
    ones_rows = jnp.ones((ONES_ROWS, TK), BF16)
    c8 = ct_ref[0, :, pl.ds(pl.multiple_of(qi * TQ, TQ), TQ)]
    c_q = [jnp.sum(jnp.where(sub == HEADS_PER_STEP * group + h, c8, 0.0), axis=0, keepdims=True)
           for h in range(HEADS_PER_STEP)]

    def scores(j, s_ref=s_ref):
        k4 = k_ref[0, _key_rows(j), :]
        for h in range(HEADS_PER_STEP):
            s_ref[h] = lax.dot_general(k4, qs_ref[h * TQ:(h + 1) * TQ, :], _NT,
                                       preferred_element_type=F32)

    def values(j, alphas, p_ref=p_ref):
        for h in range(HEADS_PER_STEP):
            lhs = jnp.concatenate([vt_ref[j, h * HEAD_DIM:(h + 1) * HEAD_DIM, :], ones_rows], axis=0)
            acc_ref[h] = alphas[h] * acc_ref[h] + jnp.dot(lhs, p_ref[h], preferred_element_type=F32)

    def softmax_block(j, m_run, s_ref=s_ref, p_ref=p_ref):
        m_next, alphas = [], []
        for h in range(HEADS_PER_STEP):
            c_k = ckb_ref[h, _key_rows(j), :]
            u = s_ref[h] - jnp.concatenate([c_k, c_k], axis=1)
            m_new = jnp.maximum(m_run[h], jnp.max(u, axis=0, keepdims=True) + c_q[h])
            alphas.append(jnp.exp2(m_run[h] - m_new))
            p_ref[h] = jnp.exp2(u - (m_new - c_q[h])).astype(BF16)
            m_next.append(m_new)
        return tuple(m_next), tuple(alphas)

    def softmax_diagonal(m_run):
        tri = causal_t_ref[:HALF, :HALF]
        m_next, alphas = [], []
        for h in range(HEADS_PER_STEP):
            c_k = ckb_ref[h, _key_rows(qi), :]
            c_top, c_bot = c_k[:HALF], c_k[HALF:]
            u_tl = s_ref[h, :HALF, :HALF] - c_top + tri
            u_tr = s_ref[h, :HALF, HALF:] - c_top
            u_br = s_ref[h, HALF:, HALF:] - c_bot + tri
            col_max = jnp.concatenate(
                [jnp.max(u_tl, axis=0, keepdims=True),
                 jnp.maximum(jnp.max(u_tr, axis=0, keepdims=True), jnp.max(u_br, axis=0, keepdims=True))], axis=1)
            m_new = jnp.maximum(m_run[h], col_max + c_q[h])
            alphas.append(jnp.exp2(m_run[h] - m_new))
            shift = m_new - c_q[h]
            p_ref[h, :HALF, :HALF] = jnp.exp2(u_tl - shift[:, :HALF]).astype(BF16)
            p_ref[h, :HALF, HALF:] = jnp.exp2(u_tr - shift[:, HALF:]).astype(BF16)
            p_ref[h, HALF:, :HALF] = jnp.zeros((HALF, HALF), BF16)
            p_ref[h, HALF:, HALF:] = jnp.exp2(u_br - shift[:, HALF:]).astype(BF16)
            m_next.append(m_new)
        return tuple(m_next), tuple(alphas)

    return scores, values, softmax_block, softmax_diagonal


def _sb_stages(k_ref, v_ref, pair_mask_ref, causal_ref, tri_ref, qs_ref, z_ref, w_ref, acc_ref):
    pair_mask = pair_mask_ref[...]

    def scores(j):
        k4 = k_ref[0, _key_rows(j), :]
        for half in _HALVES:
            z_ref[half, :] = lax.dot_general(qs_ref[half, :], k4, _NT, preferred_element_type=F32)

    def values(j):
        v4 = v_ref[0, _key_rows(j), :]
        for p in range(PAIRS_PER_STEP):
            v_pair = v4[:, p * PAIR_LANES:(p + 1) * PAIR_LANES]
            vbd = jnp.concatenate([v_pair, v_pair], axis=0) * pair_mask
            acc_ref[p] += jnp.dot(w_ref[p], vbd, preferred_element_type=F32)

    def log_beta_and_softplus(z):
        log_beta = jnp.minimum(z, 0.0) - jnp.log(1.0 + jnp.exp2(jnp.abs(z) * -LOG2E))
        return log_beta, (z - log_beta).astype(BF16)

    def suffix_sums(sps):
        csums = [jnp.dot(jnp.concatenate(sps[2 * p:2 * p + 2], axis=0), tri_ref[...],
                         preferred_element_type=F32) for p in range(PAIRS_PER_STEP)]
        return [csums[h // 2][(h % 2) * TQ:(h % 2 + 1) * TQ] for h in range(HEADS_PER_STEP)]

    def extend(carry, csums, sps):
        return tuple(carry[h] + (csums[h][:, 0:1] + sps[h][:, 0:1].astype(F32)) for h in range(HEADS_PER_STEP))

    def weights(carry):
        log_betas, sps = zip(*(log_beta_and_softplus(z_ref[h * TQ:(h + 1) * TQ, :])
                               for h in range(HEADS_PER_STEP)))
        csums = suffix_sums(sps)
        for h in range(HEADS_PER_STEP):
            w_ref[h // 2, :, (h % 2) * TK:(h % 2 + 1) * TK] = (
                jnp.exp(log_betas[h] - csums[h] - carry[h]).astype(BF16))
        return extend(carry, csums, sps)

    def weights_diagonal(carry):
        tri = causal_ref[:HALF, :HALF]
        rows = lambda h, lo: slice(h * TQ + lo, h * TQ + lo + HALF)
        lb_tl, lb_bot, sps = [], [], []
        for h in range(HEADS_PER_STEP):
            lb, sp_tl = log_beta_and_softplus(z_ref[rows(h, 0), :HALF] + tri)
            lb_tl.append(lb)
            lb_bl, sp_bl = log_beta_and_softplus(z_ref[rows(h, HALF), :HALF])
            lb_br, sp_br = log_beta_and_softplus(z_ref[rows(h, HALF), HALF:] + tri)
            lb_bot.append(jnp.concatenate([lb_bl, lb_br], axis=1))
            sps.append(jnp.concatenate([jnp.concatenate([sp_tl, jnp.zeros_like(sp_tl)], axis=1),
                                        jnp.concatenate([sp_bl, sp_br], axis=1)], axis=0))
        csums = suffix_sums(sps)
        for h in range(HEADS_PER_STEP):
            lanes = slice((h % 2) * TK, (h % 2 + 1) * TK)
            left = slice((h % 2) * TK, (h % 2) * TK + HALF)
            right = slice((h % 2) * TK + HALF, (h % 2 + 1) * TK)
            w_ref[h // 2, :HALF, left] = jnp.exp(lb_tl[h] - csums[h][:HALF, :HALF] - carry[h][:HALF]).astype(BF16)
            w_ref[h // 2, :HALF, right] = jnp.zeros((HALF, HALF), BF16)
            w_ref[h // 2, HALF:, lanes] = jnp.exp(lb_bot[h] - csums[h][HALF:] - carry[h][HALF:]).astype(BF16)
        return extend(carry, csums, sps)

    return scores, values, weights, weights_diagonal


def _attn_kernel(fq_ref, fk_ref, fv_ref, ctok_ref, ct_ref, sq_ref, sk_ref, sv_ref,
                 head_mask_ref, fox_causal_ref, pair_mask_ref, sb_causal_ref, tri_ref,
                 fo_ref, so_ref,
                 fqs_ref, vt_ref, ckb_ref, s_ref, p_ref, s1_ref, p1_ref, facc_ref, sqs_ref, z_ref, w_ref, sacc_ref):
    group = pl.program_id(1)
    qi = pl.program_id(2)

    @pl.when(qi == 0)
    def _():
        lane = lax.broadcasted_iota(jnp.int32, (TK, LANES), 1)
        for j in range(fk_ref.shape[1] // TK):
            rows = slice(j * TK, (j + 1) * TK)
            vt_ref[j] = jnp.transpose(fv_ref[0, rows, :].astype(F32)).astype(BF16)
            ctok = ctok_ref[0, rows, :]
            for h in range(HEADS_PER_STEP):
                c_col = jnp.sum(jnp.where(lane == HEADS_PER_STEP * group + h, ctok, 0.0),
                                axis=-1, keepdims=True)
                ckb_ref[h, rows, :] = jnp.broadcast_to(c_col, (TK, LANES))

    _stack_masked_queries(fq_ref, head_mask_ref, fqs_ref)
    _stack_masked_queries(sq_ref, head_mask_ref, sqs_ref)
    facc_ref[...] = jnp.zeros(facc_ref.shape, F32)
    sacc_ref[...] = jnp.zeros(sacc_ref.shape, F32)
    fox_scores, fox_values, fox_softmax, fox_softmax_diagonal = _fox_stages(
        group, qi, fk_ref, ct_ref, fox_causal_ref, fqs_ref, vt_ref, ckb_ref, s_ref, p_ref, facc_ref)
    sb_scores, sb_values, sb_weights, sb_weights_diagonal = _sb_stages(
        sk_ref, sv_ref, pair_mask_ref, sb_causal_ref, tri_ref, sqs_ref, z_ref, w_ref, sacc_ref)

    def still_live(carry):
        lowest = functools.reduce(jnp.minimum, carry)
        return jnp.min(lowest) < SB_DEAD_CARRY

    fox_scores(qi)
    sb_scores(qi)
    m_init = tuple(jnp.full((1, TQ), NEG_BIG, F32) for _ in range(HEADS_PER_STEP))
    fox_state = fox_softmax_diagonal(m_init)
    carry = sb_weights_diagonal(tuple(jnp.zeros((TQ, 1), F32) for _ in range(HEADS_PER_STEP)))
    nxt = jnp.maximum(qi - 1, 0)
    fox_scores(nxt)
    sb_scores(nxt)

    def fox_step(t, state):
        m_run, alphas = state
        fox_values(qi - t + 1, alphas)
        state = fox_softmax(qi - t, m_run)
        fox_scores(jnp.maximum(qi - t - 1, 0))
        return state

    def both_step(state):
        t, (m_run, alphas), carry, _ = state
        fox_values(qi - t + 1, alphas)
        sb_values(qi - t + 1)
        fox_state = fox_softmax(qi - t, m_run)
        carry = sb_weights(carry)
        nxt = jnp.maximum(qi - t - 1, 0)
        fox_scores(nxt)
        sb_scores(nxt)
        return t + 1, fox_state, carry, still_live(carry)

    t_sb, fox_state, _, _ = lax.while_loop(lambda st: (st[0] <= qi) & st[3], both_step,
                                           (jnp.int32(1), fox_state, carry, jnp.bool_(True)))

    def fox_two_steps(i, state):
        t = t_sb + 2 * i
        m_run, alphas = state
        fox_values(qi - t + 1, alphas)
        m_run, alphas = fox_softmax(qi - t, m_run, s_ref, p1_ref)
        fox_scores(jnp.maximum(qi - t - 1, 0), s1_ref)
        fox_values(qi - t, alphas, p1_ref)
        state = fox_softmax(qi - t - 1, m_run, s1_ref, p_ref)
        fox_scores(jnp.maximum(qi - t - 2, 0))
        return state

    pairs = (qi + 1 - t_sb) // 2
    fox_state = lax.fori_loop(0, pairs, fox_two_steps, fox_state)
    _, alphas = lax.fori_loop(t_sb + 2 * pairs, qi + 1, fox_step, fox_state)

    fox_values(0, alphas)
    sb_values(qi - t_sb + 1)
    out_t = jnp.concatenate([facc_ref[h, :HEAD_DIM, :] / facc_ref[h, HEAD_DIM:HEAD_DIM + 1, :]
                             for h in range(HEADS_PER_STEP)], axis=0)
    fo_ref[0] = jnp.transpose(out_t).astype(BF16)
    so_ref[0] = jnp.concatenate([sacc_ref[p] for p in range(PAIRS_PER_STEP)], axis=1).astype(BF16)


def _attention(qkv_fox, qkv_sb, ctok, ct):
    b, s_len, _ = qkv_fox.shape
    groups = D_ATT // STEP_LANES
    q_spec = pl.BlockSpec((1, TQ, STEP_LANES), lambda b, g, i: (b, i, g))
    k_spec = pl.BlockSpec((1, s_len, STEP_LANES), lambda b, g, i: (b, 0, groups + g))
    v_spec = pl.BlockSpec((1, s_len, STEP_LANES), lambda b, g, i: (b, 0, 2 * groups + g))
    o_spec = pl.BlockSpec((1, TQ, STEP_LANES), lambda b, g, i: (b, i, g))
    tri = jnp.asarray(np.tril(np.ones((TK, TK), np.float32), -1), BF16)
    consts = (_head_mask(), _causal_bias(strict=False, keys_on_rows=True), _pair_mask(),
              _causal_bias(strict=True, keys_on_rows=False), tri)
    out = jax.ShapeDtypeStruct((b, s_len, D_ATT), BF16)
    return pl.pallas_call(
        _attn_kernel,
        grid=(b, groups, s_len // TQ),
        in_specs=[q_spec, k_spec, v_spec,
                  pl.BlockSpec((1, s_len, LANES), lambda b, g, i: (b, 0, 0)),
                  pl.BlockSpec((1, SUBLANES, s_len), lambda b, g, i: (b, 0, 0)),
                  q_spec, k_spec, v_spec] + [_const_spec(c.shape) for c in consts],
        out_specs=[o_spec, o_spec],
        out_shape=[out, out],
        scratch_shapes=[pltpu.VMEM((HEADS_PER_STEP * TQ, STEP_LANES), BF16),
                        pltpu.VMEM((s_len // TK, STEP_LANES, TK), BF16),
                        pltpu.VMEM((HEADS_PER_STEP, s_len, LANES), F32),
                        pltpu.VMEM((HEADS_PER_STEP, TK, TQ), F32),
                        pltpu.VMEM((HEADS_PER_STEP, TK, TQ), BF16),
                        pltpu.VMEM((HEADS_PER_STEP, TK, TQ), F32),
                        pltpu.VMEM((HEADS_PER_STEP, TK, TQ), BF16),
                        pltpu.VMEM((HEADS_PER_STEP, HEAD_DIM + ONES_ROWS, TQ), F32),
                        pltpu.VMEM((HEADS_PER_STEP * TQ, STEP_LANES), BF16),
                        pltpu.VMEM((HEADS_PER_STEP * TQ, TK), F32),
                        pltpu.VMEM((PAIRS_PER_STEP, TQ, 2 * TK), BF16),
                        pltpu.VMEM((PAIRS_PER_STEP, TQ, PAIR_LANES), F32)],
        compiler_params=pltpu.CompilerParams(dimension_semantics=("arbitrary",) * 3),
        name="attention",
    )(qkv_fox, qkv_fox, qkv_fox, ctok, ct, qkv_sb, qkv_sb, qkv_sb, *consts)


def _post_kernel(x_ref, of_ref, os_ref, gl_ref, p_ref, bg_ref, wbf_ref, wbs_ref, wout_ref,
                 gmlp_ref, wup_ref, wdown_ref, gple_ref, wpg_ref, wple_ref, gfin_ref, out_ref):
    o_fox = jnp.dot(of_ref[...], wbf_ref[...], preferred_element_type=F32)
    o_sb = jnp.dot(os_ref[...], wbs_ref[...], preferred_element_type=F32)
    gate_a = jax.nn.sigmoid(gl_ref[:, :D_MODEL].astype(F32) + bg_ref[0:1, :])
    gate_b = jax.nn.sigmoid(gl_ref[:, D_MODEL:].astype(F32) + bg_ref[1:2, :])
    merged = (gate_a * o_fox + gate_b * o_sb).astype(BF16)
    x1 = x_ref[...] + jnp.dot(merged, wout_ref[...], preferred_element_type=F32)

    h = _rms_scale(x1, gmlp_ref[...]).astype(BF16)
    mlp = jnp.zeros_like(x1)
    for c in range(D_FF // FF_CHUNK):
        cols = slice(c * FF_CHUNK, (c + 1) * FF_CHUNK)
        up = jnp.maximum(jnp.dot(h, wup_ref[:, cols], preferred_element_type=F32), 0.0)
        mlp = mlp + jnp.dot((up * up).astype(BF16), wdown_ref[cols, :], preferred_element_type=F32)
    x2 = x1 + mlp

    h = _rms_scale(x2, gple_ref[...]).astype(BF16)
    gate = jax.nn.sigmoid(jnp.dot(h, wpg_ref[...], preferred_element_type=F32))
    emb = jnp.dot(p_ref[...].astype(BF16), wple_ref[...], preferred_element_type=F32)
    x3 = x2 + gate * emb
    out_ref[...] = _rms_scale(x3, gfin_ref[...])


def _post(x2, o_fox, o_sb, gl, p2, consts):
    t = x2.shape[0]
    tm = TM_POST
    row = lambda n: pl.BlockSpec((tm, n), lambda i: (i, 0))
    w_bytes = sum(int(c.size) * c.dtype.itemsize for c in consts)
    tile_bytes = tm * (4 * D_MODEL + 2 * D_ATT + 2 * D_ATT + 2 * 2 * D_MODEL + 4 * D_PLE + 4 * D_MODEL)
    scratch_bytes = tm * (6 * 4 * D_MODEL + 6 * FF_CHUNK)
    return pl.pallas_call(
        _post_kernel,
        grid=(t // tm,),
        in_specs=[row(D_MODEL), row(D_ATT), row(D_ATT), row(2 * D_MODEL), row(D_PLE)]
                 + [_const_spec(c.shape) for c in consts],
        out_specs=row(D_MODEL),
        out_shape=jax.ShapeDtypeStruct((t, D_MODEL), F32),
        compiler_params=pltpu.CompilerParams(
            dimension_semantics=("arbitrary",),
            vmem_limit_bytes=w_bytes + 2 * tile_bytes + scratch_bytes),
        name="post",
    )(x2, o_fox, o_sb, gl, p2, *consts)


def kernel(x, p, g_mix, w_in, b_forget, b_gate, w_branch_fox, w_branch_sb, w_out,
           g_mlp, w_up, w_down, g_ple, w_ple_gate, w_ple, g_final):
    b, s_len, d = x.shape
    t = b * s_len
    depth = w_in.shape[0]
    assert depth == 1, "the post kernel fuses the final RMSNorm into the (single) layer"
    x2 = x.reshape(t, d)
    for i in range(depth):
        wa, wf, wb, wg = _cast_w_in(w_in[i])
        b_pad = jnp.pad(b_forget[i], (0, LANES - N_HEADS)).reshape(1, LANES)

        qkv_fox, f_log, qkv_sb, gl = _in_proj(x2, g_mix[i].reshape(1, d), wa, wf, wb, wg)
        ctok, ct = _forget_cumsum(f_log.reshape(b, s_len, LANES), b_pad)
        o_fox, o_sb = _attention(qkv_fox.reshape(b, s_len, 3 * D_ATT), qkv_sb.reshape(b, s_len, 3 * D_ATT),
                                 ctok, ct)

        consts = (b_gate[i], w_branch_fox[i].astype(BF16), w_branch_sb[i].astype(BF16),
                  w_out[i].astype(BF16), g_mlp[i].reshape(1, d), w_up[i].astype(BF16),
                  w_down[i].astype(BF16), g_ple[i].reshape(1, d), w_ple_gate[i].astype(BF16),
                  w_ple[i].astype(BF16), g_final.reshape(1, d))
        x2 = _post(x2, o_fox.reshape(t, D_ATT), o_sb.reshape(t, D_ATT), gl,
                   p[i].reshape(t, D_PLE), consts)
    return x2.reshape(b, s_len, d)
```

```python
import functools

import jax
import jax.numpy as jnp
import numpy as np
from jax import lax
from jax.experimental import pallas as pl
from jax.experimental.pallas import tpu as pltpu

D_MODEL = 1024
HEAD_DIM = 64
N_HEADS = 8
D_ATT = N_HEADS * HEAD_DIM
D_FF = 4 * D_MODEL
D_PLE = 256
EPS = 1e-6
SCALE = HEAD_DIM ** -0.5
LOG2E = 1.4426950408889634

LANES = 128
SUBLANES = 8
HEADS_PER_STEP = 4
PAIRS_PER_STEP = HEADS_PER_STEP // 2
PAIR_LANES = 2 * HEAD_DIM
STEP_LANES = HEADS_PER_STEP * HEAD_DIM
NEG_BIG = -1e30
ONES_ROWS = 16
SB_DEAD_CARRY = 90.0

TM_PROJ = 512
TM_POST = 512
FF_CHUNK = 512
TQ = 256
TK = 256
HALF = TK // 2
CUM_BLK = 128

BF16 = jnp.bfloat16
F32 = jnp.float32

_NT = (((1,), (1,)), ((), ()))


def _const_spec(shape):
    return pl.BlockSpec(shape, lambda *_: (0,) * len(shape), pipeline_mode=pl.Buffered(1))


def _rms_scale(x, g):
    r = lax.rsqrt(jnp.mean(x * x, axis=-1, keepdims=True) + EPS)
    return x * r * g


def _split3(a):
    a1 = a.astype(BF16)
    r1 = a - a1.astype(F32)
    a2 = r1.astype(BF16)
    a3 = (r1 - a2.astype(F32)).astype(BF16)
    return a1, a2, a3


def _in_proj_kernel(x_ref, g_ref, wa_ref, wf_ref, wb_ref, wg_ref, b_ref,
                    qkva_ref, qkvb_ref, gl_ref, ctok_ref, ct_ref, carry_ref, *, tiles_per_seq):
    h = _rms_scale(x_ref[...], g_ref[...]).astype(BF16)
    @pl.when(pl.program_id(0) % tiles_per_seq == 0)
    def _():
        carry_ref[...] = jnp.zeros(carry_ref.shape, F32)

    f = jnp.dot(h, wf_ref[...], preferred_element_type=F32)
    row = lax.broadcasted_iota(jnp.int32, (CUM_BLK, CUM_BLK), 0)
    col = lax.broadcasted_iota(jnp.int32, (CUM_BLK, CUM_BLK), 1)
    tri = (col <= row).astype(BF16)
    n_blk = f.shape[0] // CUM_BLK
    parts = []
    for blk in range(n_blk):
        y = f[blk * CUM_BLK:(blk + 1) * CUM_BLK] + b_ref[...]
        parts.append(_split3(-(jnp.maximum(-y, 0.0) + jnp.log(1.0 + jnp.exp(-jnp.abs(y))))))

    def project(w_ref, out_ref, q_scale):
        qkv = jnp.dot(h, w_ref[...], preferred_element_type=F32)
        out_ref[:, :D_ATT] = (qkv[:, :D_ATT] * q_scale).astype(BF16)
        out_ref[:, D_ATT:] = qkv[:, D_ATT:].astype(BF16)

    project(wa_ref, qkva_ref, SCALE * LOG2E)
    in_block = [sum(jnp.dot(tri, p, preferred_element_type=F32) for p in parts[blk]) for blk in range(n_blk)]
    project(wb_ref, qkvb_ref, SCALE)
    gl_ref[...] = jnp.dot(h, wg_ref[...], preferred_element_type=F32).astype(BF16)

    carry = carry_ref[...]
    for blk in range(n_blk):
        rows = slice(blk * CUM_BLK, (blk + 1) * CUM_BLK)
        c = carry + in_block[blk]
        carry = c[CUM_BLK - 1:CUM_BLK, :]
        c2 = c * LOG2E
        ctok_ref[rows, :] = c2
        ct_ref[0, :, rows] = jnp.transpose(c2)[:SUBLANES, :]
    carry_ref[...] = carry


def _in_proj(x2, g_mix, wa, wf, wb, wg, b_pad, s_len):
    t = x2.shape[0]
    tm = TM_PROJ
    tiles_per_seq = s_len // tm
    row = lambda n: pl.BlockSpec((tm, n), lambda i: (i, 0))
    w_bytes = 2 * D_MODEL * (wa.shape[1] + wf.shape[1] + wb.shape[1] + wg.shape[1])
    tile_bytes = tm * (4 * D_MODEL + 2 * wa.shape[1] + 2 * wb.shape[1] + 2 * wg.shape[1] + 4 * LANES + 4 * SUBLANES)
    scratch_bytes = 4 * tm * wg.shape[1] + 2 * tm * D_MODEL
    return pl.pallas_call(
        functools.partial(_in_proj_kernel, tiles_per_seq=tiles_per_seq),
        grid=(t // tm,),
        in_specs=[row(D_MODEL), _const_spec(g_mix.shape), _const_spec(wa.shape), _const_spec(wf.shape),
                  _const_spec(wb.shape), _const_spec(wg.shape), _const_spec(b_pad.shape)],
        out_specs=[row(wa.shape[1]), row(wb.shape[1]), row(wg.shape[1]), row(LANES),
                   pl.BlockSpec((1, SUBLANES, tm), lambda i: (i // tiles_per_seq, 0, i % tiles_per_seq))],
        out_shape=[jax.ShapeDtypeStruct((t, wa.shape[1]), BF16),
                   jax.ShapeDtypeStruct((t, wb.shape[1]), BF16),
                   jax.ShapeDtypeStruct((t, wg.shape[1]), BF16),
                   jax.ShapeDtypeStruct((t, LANES), F32),
                   jax.ShapeDtypeStruct((t // s_len, SUBLANES, s_len), F32)],
        scratch_shapes=[pltpu.VMEM((1, LANES), F32)],
        compiler_params=pltpu.CompilerParams(
            dimension_semantics=("arbitrary",),
            vmem_limit_bytes=w_bytes + 2 * tile_bytes + 2 * scratch_bytes),
        name="in_proj",
    )(x2, g_mix, wa, wf, wb, wg, b_pad)


def _head_mask():
    lane = np.arange(STEP_LANES)[None, :] // HEAD_DIM
    head = np.repeat(np.arange(HEADS_PER_STEP), TQ)[:, None]
    return jnp.asarray((lane == head).astype(np.float32), BF16)


def _pair_mask():
    lane = np.arange(PAIR_LANES)[None, :] // HEAD_DIM
    half = np.repeat(np.arange(2), TK)[:, None]
    return jnp.asarray((lane == half).astype(np.float32), BF16)


def _causal_bias(strict, keys_on_rows):
    key_pos, q_pos = np.arange(TK)[:, None], np.arange(TQ)[None, :]
    allowed = (key_pos < q_pos) if strict else (key_pos <= q_pos)
    bias = np.where(allowed, 0.0, NEG_BIG).astype(np.float32)
    return jnp.asarray(bias if keys_on_rows else bias.T, F32)


def _stack_masked_queries(q_ref, head_mask_ref, qs_ref):
    qs_ref[...] = jnp.concatenate([q_ref[0]] * HEADS_PER_STEP, axis=0) * head_mask_ref[...]


def _key_rows(j):
    return pl.ds(pl.multiple_of(j * TK, TK), TK)


_HALVES = tuple(slice(i * PAIRS_PER_STEP * TQ, (i + 1) * PAIRS_PER_STEP * TQ) for i in range(2))


def _fox_stages(group, qi, k_ref, ct_ref, causal_t_ref, qs_ref, vt_ref, ckb_ref, s_ref, p_ref, acc_ref):
    sub = lax.broadcasted_iota(jnp.int32, (SUBLANES, TQ), 0)
    ones_rows = jnp.ones((ONES_ROWS, TK), BF16)
    c8 = ct_ref[0, :, pl.ds(pl.multiple_of(qi * TQ, TQ), TQ)]
    c_q = [jnp.sum(jnp.where(sub == HEADS_PER_STEP * group + h, c8, 0.0), axis=0, keepdims=True)
           for h in range(HEADS_PER_STEP)]

    def scores(j, s_ref=s_ref):
        k4 = k_ref[0, _key_rows(j), :]
        for h in range(HEADS_PER_STEP):
            s_ref[h] = lax.dot_general(k4, qs_ref[h * TQ:(h + 1) * TQ, :], _NT,
                                       preferred_element_type=F32)

    def values(j, alphas, p_ref=p_ref):
        for h in range(HEADS_PER_STEP):
            lhs = jnp.concatenate([vt_ref[j, h * HEAD_DIM:(h + 1) * HEAD_DIM, :], ones_rows], axis=0)
            acc_ref[h] = alphas[h] * acc_ref[h] + jnp.dot(lhs, p_ref[h], preferred_element_type=F32)

    def softmax_block(j, m_run, s_ref=s_ref, p_ref=p_ref):
        m_next, alphas = [], []
        for h in range(HEADS_PER_STEP):
            c_k = ckb_ref[h, _key_rows(j), :]
            u = s_ref[h] - jnp.concatenate([c_k, c_k], axis=1)
            m_new = jnp.maximum(m_run[h], jnp.max(u, axis=0, keepdims=True) + c_q[h])
            alphas.append(jnp.exp2(m_run[h] - m_new))
            p_ref[h] = jnp.exp2(u - (m_new - c_q[h])).astype(BF16)
            m_next.append(m_new)
        return tuple(m_next), tuple(alphas)

    def softmax_diagonal(m_run):
        tri = causal_t_ref[:HALF, :HALF]
        m_next, alphas = [], []
        for h in range(HEADS_PER_STEP):
            c_k = ckb_ref[h, _key_rows(qi), :]
            c_top, c_bot = c_k[:HALF], c_k[HALF:]
            u_tl = s_ref[h, :HALF, :HALF] - c_top + tri
            u_tr = s_ref[h, :HALF, HALF:] - c_top
            u_br = s_ref[h, HALF:, HALF:] - c_bot + tri
            col_max = jnp.concatenate(
                [jnp.max(u_tl, axis=0, keepdims=True),
                 jnp.maximum(jnp.max(u_tr, axis=0, keepdims=True), jnp.max(u_br, axis=0, keepdims=True))], axis=1)
            m_new = jnp.maximum(m_run[h], col_max + c_q[h])
            alphas.append(jnp.exp2(m_run[h] - m_new))
            shift = m_new - c_q[h]
            p_ref[h, :HALF, :HALF] = jnp.exp2(u_tl - shift[:, :HALF]).astype(BF16)
            p_ref[h, :HALF, HALF:] = jnp.exp2(u_tr - shift[:, HALF:]).astype(BF16)
            p_ref[h, HALF:, :HALF] = jnp.zeros((HALF, HALF), BF16)
            p_ref[h, HALF:, HALF:] = jnp.exp2(u_br - shift[:, HALF:]).astype(BF16)
            m_next.append(m_new)
        return tuple(m_next), tuple(alphas)

    return scores, values, softmax_block, softmax_diagonal


def _sb_stages(k_ref, v_ref, pair_mask_ref, causal_ref, tri_ref, qs_ref, z_ref, w_ref, acc_ref):
    pair_mask = pair_mask_ref[...]

    def scores(j):
        k4 = k_ref[0, _key_rows(j), :]
        for half in _HALVES:
            z_ref[half, :] = lax.dot_general(qs_ref[half, :], k4, _NT, preferred_element_type=F32)

    def values(j):
        v4 = v_ref[0, _key_rows(j), :]
        for p in range(PAIRS_PER_STEP):
            v_pair = v4[:, p * PAIR_LANES:(p + 1) * PAIR_LANES]
            vbd = jnp.concatenate([v_pair, v_pair], axis=0) * pair_mask
            acc_ref[p] += jnp.dot(w_ref[p], vbd, preferred_element_type=F32)

    def log_beta_and_softplus(z):
        log_beta = jnp.minimum(z, 0.0) - jnp.log(1.0 + jnp.exp2(jnp.abs(z) * -LOG2E))
        return log_beta, (z - log_beta).astype(BF16)

    def suffix_sums(sps):
        csums = [jnp.dot(jnp.concatenate(sps[2 * p:2 * p + 2], axis=0), tri_ref[...],
                         preferred_element_type=F32) for p in range(PAIRS_PER_STEP)]
        return [csums[h // 2][(h % 2) * TQ:(h % 2 + 1) * TQ] for h in range(HEADS_PER_STEP)]

    def extend(carry, csums, sps):
        return tuple(carry[h] + (csums[h][:, 0:1] + sps[h][:, 0:1].astype(F32)) for h in range(HEADS_PER_STEP))

    def weights(carry):
        log_betas, sps = zip(*(log_beta_and_softplus(z_ref[h * TQ:(h + 1) * TQ, :])
                               for h in range(HEADS_PER_STEP)))
        csums = suffix_sums(sps)
        for h in range(HEADS_PER_STEP):
            w_ref[h // 2, :, (h % 2) * TK:(h % 2 + 1) * TK] = (
                jnp.exp(log_betas[h] - csums[h] - carry[h]).astype(BF16))
        return extend(carry, csums, sps)

    def weights_diagonal(carry):
        tri = causal_ref[:HALF, :HALF]
        rows = lambda h, lo: slice(h * TQ + lo, h * TQ + lo + HALF)
        lb_tl, lb_bot, sps = [], [], []
        for h in range(HEADS_PER_STEP):
            lb, sp_tl = log_beta_and_softplus(z_ref[rows(h, 0), :HALF] + tri)
            lb_tl.append(lb)
            lb_bl, sp_bl = log_beta_and_softplus(z_ref[rows(h, HALF), :HALF])
            lb_br, sp_br = log_beta_and_softplus(z_ref[rows(h, HALF), HALF:] + tri)
            lb_bot.append(jnp.concatenate([lb_bl, lb_br], axis=1))
            sps.append(jnp.concatenate([jnp.concatenate([sp_tl, jnp.zeros_like(sp_tl)], axis=1),
                                        jnp.concatenate([sp_bl, sp_br], axis=1)], axis=0))
        csums = suffix_sums(sps)
        for h in range(HEADS_PER_STEP):
            lanes = slice((h % 2) * TK, (h % 2 + 1) * TK)
            left = slice((h % 2) * TK, (h % 2) * TK + HALF)
            right = slice((h % 2) * TK + HALF, (h % 2 + 1) * TK)
            w_ref[h // 2, :HALF, left] = jnp.exp(lb_tl[h] - csums[h][:HALF, :HALF] - carry[h][:HALF]).astype(BF16)
            w_ref[h // 2, :HALF, right] = jnp.zeros((HALF, HALF), BF16)
            w_ref[h // 2, HALF:, lanes] = jnp.exp(lb_bot[h] - csums[h][HALF:] - carry[h][HALF:]).astype(BF16)
        return extend(carry, csums, sps)

    return scores, values, weights, weights_diagonal


def _attn_kernel(fq_ref, fk_ref, fv_ref, ctok_ref, ct_ref, sq_ref, sk_ref, sv_ref,
                 head_mask_ref, fox_causal_ref, pair_mask_ref, sb_causal_ref, tri_ref,
                 fo_ref, so_ref,
                 fqs_ref, vt_ref, ckb_ref, s_ref, p_ref, s1_ref, p1_ref, facc_ref, sqs_ref, z_ref, w_ref, sacc_ref):
    group = pl.program_id(1)
    qi = pl.program_id(2)

    @pl.when(qi == 0)
    def _():
        lane = lax.broadcasted_iota(jnp.int32, (TK, LANES), 1)
        for j in range(fk_ref.shape[1] // TK):
            rows = slice(j * TK, (j + 1) * TK)
            vt_ref[j] = jnp.transpose(fv_ref[0, rows, :].astype(F32)).astype(BF16)
            ctok = ctok_ref[0, rows, :]
            for h in range(HEADS_PER_STEP):
                c_col = jnp.sum(jnp.where(lane == HEADS_PER_STEP * group + h, ctok, 0.0),
                                axis=-1, keepdims=True)
                ckb_ref[h, rows, :] = jnp.broadcast_to(c_col, (TK, LANES))

    _stack_masked_queries(fq_ref, head_mask_ref, fqs_ref)
    _stack_masked_queries(sq_ref, head_mask_ref, sqs_ref)
    facc_ref[...] = jnp.zeros(facc_ref.shape, F32)
    sacc_ref[...] = jnp.zeros(sacc_ref.shape, F32)
    fox_scores, fox_values, fox_softmax, fox_softmax_diagonal = _fox_stages(
        group, qi, fk_ref, ct_ref, fox_causal_ref, fqs_ref, vt_ref, ckb_ref, s_ref, p_ref, facc_ref)
    sb_scores, sb_values, sb_weights, sb_weights_diagonal = _sb_stages(
        sk_ref, sv_ref, pair_mask_ref, sb_causal_ref, tri_ref, sqs_ref, z_ref, w_ref, sacc_ref)

    def still_live(carry):
        lowest = functools.reduce(jnp.minimum, carry)
        return jnp.min(lowest) < SB_DEAD_CARRY

    fox_scores(qi)
    sb_scores(qi)
    m_init = tuple(jnp.full((1, TQ), NEG_BIG, F32) for _ in range(HEADS_PER_STEP))
    fox_state = fox_softmax_diagonal(m_init)
    carry = sb_weights_diagonal(tuple(jnp.zeros((TQ, 1), F32) for _ in range(HEADS_PER_STEP)))
    nxt = jnp.maximum(qi - 1, 0)
    fox_scores(nxt)
    sb_scores(nxt)

    def fox_step(t, state):
        m_run, alphas = state
        fox_values(qi - t + 1, alphas)
        state = fox_softmax(qi - t, m_run)
        fox_scores(jnp.maximum(qi - t - 1, 0))
        return state

    def both_step(state):
        t, (m_run, alphas), carry, _ = state
        fox_values(qi - t + 1, alphas)
        sb_values(qi - t + 1)
        fox_state = fox_softmax(qi - t, m_run)
        carry = sb_weights(carry)
        nxt = jnp.maximum(qi - t - 1, 0)
        fox_scores(nxt)
        sb_scores(nxt)
        return t + 1, fox_state, carry, still_live(carry)

    t_sb, fox_state, _, _ = lax.while_loop(lambda st: (st[0] <= qi) & st[3], both_step,
                                           (jnp.int32(1), fox_state, carry, jnp.bool_(True)))

    def fox_two_steps(i, state):
        t = t_sb + 2 * i
        m_run, alphas = state
        fox_values(qi - t + 1, alphas)
        m_run, alphas = fox_softmax(qi - t, m_run, s_ref, p1_ref)
        fox_scores(jnp.maximum(qi - t - 1, 0), s1_ref)
        fox_values(qi - t, alphas, p1_ref)
        state = fox_softmax(qi - t - 1, m_run, s1_ref, p_ref)
        fox_scores(jnp.maximum(qi - t - 2, 0))
        return state

    pairs = (qi + 1 - t_sb) // 2
    fox_state = lax.fori_loop(0, pairs, fox_two_steps, fox_state)
    _, alphas = lax.fori_loop(t_sb + 2 * pairs, qi + 1, fox_step, fox_state)

    fox_values(0, alphas)
    sb_values(qi - t_sb + 1)
    out_t = jnp.concatenate([facc_ref[h, :HEAD_DIM, :] / facc_ref[h, HEAD_DIM:HEAD_DIM + 1, :]
                             for h in range(HEADS_PER_STEP)], axis=0)
    fo_ref[0] = jnp.transpose(out_t).astype(BF16)
    so_ref[0] = jnp.concatenate([sacc_ref[p] for p in range(PAIRS_PER_STEP)], axis=1).astype(BF16)


def _attention(qkv_fox, qkv_sb, ctok, ct):
    b, s_len, _ = qkv_fox.shape
    groups = D_ATT // STEP_LANES
    q_spec = pl.BlockSpec((1, TQ, STEP_LANES), lambda b, g, i: (b, i, g))
    k_spec = pl.BlockSpec((1, s_len, STEP_LANES), lambda b, g, i: (b, 0, groups + g))
    v_spec = pl.BlockSpec((1, s_len, STEP_LANES), lambda b, g, i: (b, 0, 2 * groups + g))
    o_spec = pl.BlockSpec((1, TQ, STEP_LANES), lambda b, g, i: (b, i, g))
    tri = jnp.asarray(np.tril(np.ones((TK, TK), np.float32), -1), BF16)
    consts = (_head_mask(), _causal_bias(strict=False, keys_on_rows=True), _pair_mask(),
              _causal_bias(strict=True, keys_on_rows=False), tri)
    out = jax.ShapeDtypeStruct((b, s_len, D_ATT), BF16)
    return pl.pallas_call(
        _attn_kernel,
        grid=(b, groups, s_len // TQ),
        in_specs=[q_spec, k_spec, v_spec,
                  pl.BlockSpec((1, s_len, LANES), lambda b, g, i: (b, 0, 0)),
                  pl.BlockSpec((1, SUBLANES, s_len), lambda b, g, i: (b, 0, 0)),
                  q_spec, k_spec, v_spec] + [_const_spec(c.shape) for c in consts],
        out_specs=[o_spec, o_spec],
        out_shape=[out, out],
        scratch_shapes=[pltpu.VMEM((HEADS_PER_STEP * TQ, STEP_LANES), BF16),
                        pltpu.VMEM((s_len // TK, STEP_LANES, TK), BF16),
                        pltpu.VMEM((HEADS_PER_STEP, s_len, LANES), F32),
                        pltpu.VMEM((HEADS_PER_STEP, TK, TQ), F32),
                        pltpu.VMEM((HEADS_PER_STEP, TK, TQ), BF16),
                        pltpu.VMEM((HEADS_PER_STEP, TK, TQ), F32),
                        pltpu.VMEM((HEADS_PER_STEP, TK, TQ), BF16),
                        pltpu.VMEM((HEADS_PER_STEP, HEAD_DIM + ONES_ROWS, TQ), F32),
                        pltpu.VMEM((HEADS_PER_STEP * TQ, STEP_LANES), BF16),
                        pltpu.VMEM((HEADS_PER_STEP * TQ, TK), F32),
                        pltpu.VMEM((PAIRS_PER_STEP, TQ, 2 * TK), BF16),
                        pltpu.VMEM((PAIRS_PER_STEP, TQ, PAIR_LANES), F32)],
        compiler_params=pltpu.CompilerParams(dimension_semantics=("arbitrary",) * 3),
        name="attention",
    )(qkv_fox, qkv_fox, qkv_fox, ctok, ct, qkv_sb, qkv_sb, qkv_sb, *consts)


def _post_kernel(x_ref, of_ref, os_ref, gl_ref, p_ref, bg_ref, wbf_ref, wbs_ref, wout_ref,
                 gmlp_ref, wup_ref, wdown_ref, gple_ref, wpg_ref, wple_ref, gfin_ref, out_ref):
    o_fox = jnp.dot(of_ref[...], wbf_ref[...], preferred_element_type=F32)
    o_sb = jnp.dot(os_ref[...], wbs_ref[...], preferred_element_type=F32)
    gate_a = jax.nn.sigmoid(gl_ref[:, :D_MODEL].astype(F32) + bg_ref[0:1, :])
    gate_b = jax.nn.sigmoid(gl_ref[:, D_MODEL:].astype(F32) + bg_ref[1:2, :])
    merged = (gate_a * o_fox + gate_b * o_sb).astype(BF16)
    x1 = x_ref[...] + jnp.dot(merged, wout_ref[...], preferred_element_type=F32)

    h = _rms_scale(x1, gmlp_ref[...]).astype(BF16)
    mlp = jnp.zeros_like(x1)
    for c in range(D_FF // FF_CHUNK):
        cols = slice(c * FF_CHUNK, (c + 1) * FF_CHUNK)
        up = jnp.maximum(jnp.dot(h, wup_ref[:, cols], preferred_element_type=F32), 0.0)
        mlp = mlp + jnp.dot((up * up).astype(BF16), wdown_ref[cols, :], preferred_element_type=F32)
    x2 = x1 + mlp

    h = _rms_scale(x2, gple_ref[...]).astype(BF16)
    gate = jax.nn.sigmoid(jnp.dot(h, wpg_ref[...], preferred_element_type=F32))
    emb = jnp.dot(p_ref[...].astype(BF16), wple_ref[...], preferred_element_type=F32)
    x3 = x2 + gate * emb
    out_ref[...] = _rms_scale(x3, gfin_ref[...])


def _post(x2, o_fox, o_sb, gl, p2, consts):
    t = x2.shape[0]
    tm = TM_POST
    row = lambda n: pl.BlockSpec((tm, n), lambda i: (i, 0))
    w_bytes = sum(int(c.size) * c.dtype.itemsize for c in consts)
    tile_bytes = tm * (4 * D_MODEL + 2 * D_ATT + 2 * D_ATT + 2 * 2 * D_MODEL + 4 * D_PLE + 4 * D_MODEL)
    scratch_bytes = tm * (6 * 4 * D_MODEL + 6 * FF_CHUNK)
    return pl.pallas_call(
        _post_kernel,
        grid=(t // tm,),
        in_specs=[row(D_MODEL), row(D_ATT), row(D_ATT), row(2 * D_MODEL), row(D_PLE)]
                 + [_const_spec(c.shape) for c in consts],
        out_specs=row(D_MODEL),
        out_shape=jax.ShapeDtypeStruct((t, D_MODEL), F32),
        compiler_params=pltpu.CompilerParams(
            dimension_semantics=("arbitrary",),
            vmem_limit_bytes=w_bytes + 2 * tile_bytes + scratch_bytes),
        name="post",
    )(x2, o_fox, o_sb, gl, p2, *consts)


def kernel(x, p, g_mix, w_in, b_forget, b_gate, w_branch_fox, w_branch_sb, w_out,
           g_mlp, w_up, w_down, g_ple, w_ple_gate, w_ple, g_final):
    b, s_len, d = x.shape
    t = b * s_len
    depth = w_in.shape[0]
    assert depth == 1, "the post kernel fuses the final RMSNorm into the (single) layer"
    x2 = x.reshape(t, d)
    for i in range(depth):
        o0, o1, o2 = 3 * D_ATT, 3 * D_ATT + N_HEADS, 6 * D_ATT + N_HEADS
        wa, wf, wb, wg = (w_in[i][:, lo:hi].astype(BF16) for lo, hi in ((0, o0), (o0, o1), (o1, o2), (o2, None)))
        wf = jnp.pad(wf, ((0, 0), (0, LANES - N_HEADS)))
        b_pad = jnp.pad(b_forget[i], (0, LANES - N_HEADS)).reshape(1, LANES)

        qkv_fox, qkv_sb, gl, ctok, ct = _in_proj(x2, g_mix[i].reshape(1, d), wa, wf, wb, wg, b_pad, s_len)
        ctok = ctok.reshape(b, s_len, LANES)
        o_fox, o_sb = _attention(qkv_fox.reshape(b, s_len, 3 * D_ATT), qkv_sb.reshape(b, s_len, 3 * D_ATT),
                                 ctok, ct)

        consts = (b_gate[i], w_branch_fox[i].astype(BF16), w_branch_sb[i].astype(BF16),
                  w_out[i].astype(BF16), g_mlp[i].reshape(1, d), w_up[i].astype(BF16),
                  w_down[i].astype(BF16), g_ple[i].reshape(1, d), w_ple_gate[i].astype(BF16),
                  w_ple[i].astype(BF16), g_final.reshape(1, d))
        x2 = _post(x2, o_fox.reshape(t, D_ATT), o_sb.reshape(t, D_ATT), gl,
                   p[i].reshape(t, D_PLE), consts)
    return x2.reshape(b, s_len, d)
```
